```python
import math
import jax, jax.numpy as jnp
from jax import lax
import numpy as np

D_MODEL = 2048
BATCH = 8
SEQ = 2048
DEPTH = 1
DEC_BATCH = 16
DEC_SEQ = 2048
PAST_LEN = 128

D_MIX = 2048
ATTN_HEADS = 8
ATTN_KV_HEADS = 2
ATTN_HEAD_DIM = 128
ATTN_WIDTH = ATTN_HEADS * ATTN_HEAD_DIM
KV_WIDTH = ATTN_KV_HEADS * ATTN_HEAD_DIM
MLSTM_HEADS = 4
MLSTM_QK_DIM = 128
MLSTM_V_DIM = 256
MLSTM_QK_WIDTH = MLSTM_HEADS * MLSTM_QK_DIM
MLSTM_WIDTH = MLSTM_HEADS * MLSTM_V_DIM
N_GATE_COLS = 4 * MLSTM_HEADS
GRID_W = 64
ROPE_THETA = 10000.0
Q_BLOCK = 128
CHUNK = 128
EPS = 1e-6
IN_SPLITS = (ATTN_WIDTH, KV_WIDTH, KV_WIDTH, ATTN_WIDTH,
             MLSTM_QK_WIDTH, MLSTM_QK_WIDTH, MLSTM_WIDTH, MLSTM_WIDTH, MLSTM_WIDTH,
             N_GATE_COLS)
D_IN_PROJ = ATTN_WIDTH + 2 * KV_WIDTH + ATTN_WIDTH + 2 * MLSTM_QK_WIDTH + 3 * MLSTM_WIDTH + N_GATE_COLS

kernel_name = "hymba_attn_mlstm_bidir_encoder"


def rmsnorm(x, g):
    xf = x.astype(jnp.float32)
    y = xf * lax.rsqrt(jnp.mean(xf * xf, axis=-1, keepdims=True) + EPS) * g.astype(jnp.float32)
    return y.astype(x.dtype)


def axial_rope_tables(S):
    rows = S // GRID_W
    row = jnp.repeat(jnp.arange(rows), GRID_W).astype(jnp.float32)
    col = jnp.tile(jnp.arange(GRID_W), rows).astype(jnp.float32)
    nf = ATTN_HEAD_DIM // 4
    inv = 1.0 / (ROPE_THETA ** (jnp.arange(nf, dtype=jnp.float32) / nf))
    ang_r = row[:, None] * inv
    ang_c = col[:, None] * inv
    ang = jnp.concatenate([ang_r, ang_r, ang_c, ang_c], axis=-1)
    return jnp.cos(ang), jnp.sin(ang)


def apply_axial_rope(x, cos, sin):
    q = ATTN_HEAD_DIM // 4
    x1, x2, x3, x4 = x[..., :q], x[..., q:2 * q], x[..., 2 * q:3 * q], x[..., 3 * q:]
    rot = jnp.concatenate([-x2, x1, -x4, x3], axis=-1)
    return x * cos[None, :, None, :] + rot * sin[None, :, None, :]


def blocked_gqa_attention(q, k, v):
    B, S, H, D = q.shape
    G = H // ATTN_KV_HEADS
    nblk = S // Q_BLOCK
    scale = 1.0 / math.sqrt(D)
    qb = q.reshape(B, nblk, Q_BLOCK, ATTN_KV_HEADS, G, D).transpose(1, 0, 2, 3, 4, 5)

    def one_block(qi):
        s = jnp.einsum('bqkgd,bskd->bkgqs', qi, k) * scale
        p = jax.nn.softmax(s, axis=-1)
        return jnp.einsum('bkgqs,bskd->bqkgd', p, v)

    o = lax.map(one_block, qb)
    return o.transpose(1, 0, 2, 3, 4, 5).reshape(B, S, H * D)


def mlstm_chunkwise(q, k, v, i_pre, logf):
    B, H, S, dk = q.shape
    dv = v.shape[-1]
    nc = S // CHUNK

    def to_chunks(a):
        return jnp.moveaxis(a.reshape(a.shape[:2] + (nc, CHUNK) + a.shape[3:]), 2, 0)

    qc, kc, vc, ic, fc = (to_chunks(a) for a in (q, k, v, i_pre, logf))
    mask = jnp.tril(jnp.ones((CHUNK, CHUNK), dtype=bool))

    def step(carry, inp):
        C, n, m = carry
        qq, kk, vv, ii, ff = inp
        b = jnp.cumsum(ff, axis=-1)
        g = b[..., -1]
        a = b[..., :, None] - b[..., None, :] + ii[..., None, :]
        a = jnp.where(mask, a, -jnp.inf)
        inter = b + m[..., None]
        m_t = jnp.maximum(inter, jnp.max(a, axis=-1))
        w = jnp.exp(a - m_t[..., None])
        si = jnp.exp(inter - m_t)
        qk = jnp.einsum('bhtd,bhsd->bhts', qq, kk) * w
        num = jnp.einsum('bhts,bhsv->bhtv', qk, vv) + si[..., None] * jnp.einsum('bhtd,bhdv->bhtv', qq, C)
        den = jnp.sum(qk, axis=-1) + si * jnp.einsum('bhtd,bhd->bht', qq, n)
        h = num / jnp.maximum(jnp.abs(den), jnp.exp(-m_t))[..., None]
        e = g[..., None] - b + ii
        m_new = jnp.maximum(g + m, jnp.max(e, axis=-1))
        sc = jnp.exp(g + m - m_new)
        we = jnp.exp(e - m_new[..., None])
        C_new = sc[..., None, None] * C + jnp.einsum('bhs,bhsd,bhsv->bhdv', we, kk, vv)
        n_new = sc[..., None] * n + jnp.einsum('bhs,bhsd->bhd', we, kk)
        return (C_new, n_new, m_new), h

    init = (jnp.zeros((B, H, dk, dv), jnp.float32), jnp.zeros((B, H, dk), jnp.float32),
            jnp.zeros((B, H), jnp.float32))
    _, hs = lax.scan(step, init, (qc, kc, vc, ic, fc))
    return jnp.moveaxis(hs, 0, 2).reshape(B, H, S, dv)


def hybrid_layer(x, norm_g, w_in, b_gates, q_norm_g, k_norm_g, mlstm_norm_g, w_out):
    B, S, _ = x.shape
    f32 = jnp.float32
    h = rmsnorm(x, norm_g)
    proj = h @ w_in
    parts = []
    off = 0
    for size in IN_SPLITS:
        parts.append(proj[..., off:off + size])
        off += size
    aq, ak, av, az, mq, mk, mv, mo, mz, gates = parts

    cos, sin = axial_rope_tables(S)
    aq = rmsnorm(aq.reshape(B, S, ATTN_HEADS, ATTN_HEAD_DIM), q_norm_g).astype(f32)
    ak = rmsnorm(ak.reshape(B, S, ATTN_KV_HEADS, ATTN_HEAD_DIM), k_norm_g).astype(f32)
    av = av.reshape(B, S, ATTN_KV_HEADS, ATTN_HEAD_DIM).astype(f32)
    aq = apply_axial_rope(aq, cos, sin)
    ak = apply_axial_rope(ak, cos, sin)
    attn_out = blocked_gqa_attention(aq, ak, av).astype(x.dtype) * jax.nn.silu(az)

    g4 = (gates.astype(f32) + b_gates.astype(f32)).reshape(B, S, 4, MLSTM_HEADS)
    g4 = jnp.transpose(g4, (2, 0, 3, 1))
    i_f, f_f, i_b, f_b = g4[0], g4[1], g4[2], g4[3]
    mq = mq.astype(f32).reshape(B, S, MLSTM_HEADS, MLSTM_QK_DIM).transpose(0, 2, 1, 3) * (MLSTM_QK_DIM ** -0.5)
    mk = mk.astype(f32).reshape(B, S, MLSTM_HEADS, MLSTM_QK_DIM).transpose(0, 2, 1, 3)
    mv = mv.astype(f32).reshape(B, S, MLSTM_HEADS, MLSTM_V_DIM).transpose(0, 2, 1, 3)
    h_fwd = mlstm_chunkwise(mq, mk, mv, i_f, jax.nn.log_sigmoid(f_f))
    fl = lambda a: jnp.flip(a, axis=2)
    h_bwd = fl(mlstm_chunkwise(fl(mq), fl(mk), fl(mv), fl(i_b), fl(jax.nn.log_sigmoid(f_b))))
    hm = (h_fwd + h_bwd).transpose(0, 2, 1, 3)
    hm = jax.nn.sigmoid(mo.astype(f32)).reshape(B, S, MLSTM_HEADS, MLSTM_V_DIM) * hm
    hm = rmsnorm(hm, mlstm_norm_g.reshape(MLSTM_HEADS, MLSTM_V_DIM))
    mlstm_out = hm.reshape(B, S, MLSTM_WIDTH).astype(x.dtype) * jax.nn.silu(mz)

    y = jnp.concatenate([attn_out, mlstm_out], axis=-1) @ w_out
    return x + y.astype(x.dtype)


def trunk(x, norm_g, w_in, b_gates, q_norm_g, k_norm_g, mlstm_norm_g, w_out):
    for l in range(DEPTH):
        x = hybrid_layer(x, norm_g[l], w_in[l], b_gates[l], q_norm_g[l], k_norm_g[l],
                         mlstm_norm_g[l], w_out[l])
    return x


def setup_inputs(seed: int = 0) -> dict:
    key = jax.random.key(seed)
    ks = jax.random.split(key, 9)
    x_prompt = jax.random.normal(ks[0], (BATCH, SEQ, D_MODEL), jnp.float32)
    x_sample = jax.random.normal(ks[1], (DEC_BATCH, DEC_SEQ, D_MODEL), jnp.float32)
    norm_g = 1.0 + 0.02 * jax.random.normal(ks[2], (DEPTH, D_MODEL), jnp.float32)
    w_in = jax.random.normal(ks[3], (DEPTH, D_MODEL, D_IN_PROJ), jnp.float32) * (D_MODEL ** -0.5)
    fb = jnp.linspace(3.0, 6.0, MLSTM_HEADS, dtype=jnp.float32)
    zb = jnp.zeros((MLSTM_HEADS,), jnp.float32)
    gate_base = jnp.concatenate([zb, fb, zb, fb])
    b_gates = gate_base[None, :] + 0.1 * jax.random.normal(ks[4], (DEPTH, N_GATE_COLS), jnp.float32)
    q_norm_g = 1.0 + 0.02 * jax.random.normal(ks[5], (DEPTH, ATTN_HEAD_DIM), jnp.float32)
    k_norm_g = 1.0 + 0.02 * jax.random.normal(ks[6], (DEPTH, ATTN_HEAD_DIM), jnp.float32)
    mlstm_norm_g = 1.0 + 0.02 * jax.random.normal(ks[7], (DEPTH, MLSTM_WIDTH), jnp.float32)
    w_out = jax.random.normal(ks[8], (DEPTH, D_MIX, D_MODEL), jnp.float32) * (D_MIX ** -0.5)
    return {"x_prompt": x_prompt, "x_sample": x_sample, "norm_g": norm_g, "w_in": w_in,
            "b_gates": b_gates, "q_norm_g": q_norm_g, "k_norm_g": k_norm_g,
            "mlstm_norm_g": mlstm_norm_g, "w_out": w_out}


def reference(x_prompt, x_sample, norm_g, w_in, b_gates, q_norm_g, k_norm_g, mlstm_norm_g, w_out):
    y_prompt = trunk(x_prompt, norm_g, w_in, b_gates, q_norm_g, k_norm_g, mlstm_norm_g, w_out)
    y_sample = trunk(x_sample, norm_g, w_in, b_gates, q_norm_g, k_norm_g, mlstm_norm_g, w_out)
    return (y_prompt, y_sample)
```

```python
import functools
import math

import jax
import jax.numpy as jnp
from jax import lax
from jax.experimental import pallas as pl
from jax.experimental.pallas import tpu as pltpu

F32 = jnp.float32
BF16 = jnp.bfloat16

ATTN_HEADS = 8
ATTN_KV_HEADS = 2
HEAD_DIM = 128
ATTN_WIDTH = ATTN_HEADS * HEAD_DIM
KV_WIDTH = ATTN_KV_HEADS * HEAD_DIM
GROUP = ATTN_HEADS // ATTN_KV_HEADS
MLSTM_HEADS = 4
MLSTM_QK_DIM = 128
MLSTM_V_DIM = 256
MLSTM_QK_WIDTH = MLSTM_HEADS * MLSTM_QK_DIM
MLSTM_WIDTH = MLSTM_HEADS * MLSTM_V_DIM
N_GATE_COLS = 4 * MLSTM_HEADS
GRID_W = 64
ROPE_THETA = 10000.0
CHUNK = 128
EPS = 1e-6

OFF_AQ = 0
OFF_AK = OFF_AQ + ATTN_WIDTH
OFF_AV = OFF_AK + KV_WIDTH
OFF_AZ = OFF_AV + KV_WIDTH
OFF_MQ = OFF_AZ + ATTN_WIDTH
OFF_MK = OFF_MQ + MLSTM_QK_WIDTH
OFF_MV = OFF_MK + MLSTM_QK_WIDTH
OFF_MO = OFF_MV + MLSTM_WIDTH
OFF_MZ = OFF_MO + MLSTM_WIDTH
OFF_GATES = OFF_MZ + MLSTM_WIDTH
N_MAIN = OFF_GATES

LANES = 128
VMEM_LIMIT = 56 * 1024 * 1024


def _cparams(sem):
    return pltpu.CompilerParams(dimension_semantics=sem, vmem_limit_bytes=VMEM_LIMIT)


def _inproj_kernel(x_ref, g_ref, w_ref, wg_ref, o_ref, og_ref, h_ref, *, row_chunk):
    j = pl.program_id(1)
    tm = x_ref.shape[0]

    @pl.when(j == 0)
    def _():
        def body(r, carry):
            rows = pl.ds(pl.multiple_of(r * row_chunk, row_chunk), row_chunk)
            x = x_ref[rows, :]
            ms = jnp.mean(x * x, axis=-1, keepdims=True)
            h_ref[rows, :] = (x * lax.rsqrt(ms + EPS) * g_ref[...]).astype(BF16)
            return carry
        lax.fori_loop(0, tm // row_chunk, body, 0)
        og_ref[...] = jnp.dot(h_ref[...], wg_ref[...], preferred_element_type=F32)

    o_ref[...] = jnp.dot(h_ref[...], w_ref[...], preferred_element_type=F32).astype(BF16)


def _in_projection(x2, norm_g, w_main, w_gates, *, tm, tn):
    T, D = x2.shape
    N = w_main.shape[1]
    return pl.pallas_call(
        functools.partial(_inproj_kernel, row_chunk=64),
        grid=(T // tm, N // tn),
        in_specs=[
            pl.BlockSpec((tm, D), lambda i, j: (i, 0)),
            pl.BlockSpec((1, D), lambda i, j: (0, 0)),
            pl.BlockSpec((D, tn), lambda i, j: (0, j)),
            pl.BlockSpec((D, LANES), lambda i, j: (0, 0)),
        ],
        out_specs=[
            pl.BlockSpec((tm, tn), lambda i, j: (i, j)),
            pl.BlockSpec((tm, LANES), lambda i, j: (i, 0)),
        ],
        out_shape=[
            jax.ShapeDtypeStruct((T, N), BF16),
            jax.ShapeDtypeStruct((T, LANES), F32),
        ],
        scratch_shapes=[pltpu.VMEM((tm, D), BF16)],
        compiler_params=_cparams(("parallel", "arbitrary")),
        name="inproj",
    )(x2, norm_g, w_main, w_gates)


def _norm_rope(x, gain, cos, sin_a, sin_b):
    ms = jnp.mean(x * x, axis=-1, keepdims=True)
    xn = x * lax.rsqrt(ms + EPS) * gain
    up = pltpu.roll(xn, 3 * HEAD_DIM // 4, axis=1)
    down = pltpu.roll(xn, HEAD_DIM // 4, axis=1)
    return xn * cos + up * sin_a + down * sin_b


def _attn_kernel(q_ref, k_ref, v_ref, z_ref, cos_ref, sa_ref, sb_ref, qg_ref, kg_ref,
                 o_ref, kn_ref, *, tq):
    qi = pl.program_id(2)

    @pl.when(qi == 0)
    def _():
        k = k_ref[...].astype(F32)
        kn_ref[...] = _norm_rope(k, kg_ref[...], cos_ref[...], sa_ref[...], sb_ref[...]).astype(BF16)

    rows = pl.ds(pl.multiple_of(qi * tq, tq), tq)
    cos = cos_ref[rows, :]
    sa = sa_ref[rows, :]
    sb = sb_ref[rows, :]
    scale = 1.0 / math.sqrt(HEAD_DIM)
    for g in range(GROUP):
        cols = slice(g * HEAD_DIM, (g + 1) * HEAD_DIM)
        q = _norm_rope(q_ref[:, cols].astype(F32), qg_ref[...], cos, sa, sb)
        s = lax.dot_general(q.astype(BF16), kn_ref[...], (((1,), (1,)), ((), ())),
                            preferred_element_type=F32) * scale
        m = jnp.max(s, axis=-1, keepdims=True)
        p = jnp.exp(s - m)
        l = jnp.sum(p, axis=-1, keepdims=True)
        o = jnp.dot(p.astype(BF16), v_ref[...], preferred_element_type=F32) / l
        z = z_ref[:, cols].astype(F32)
        o_ref[:, cols] = (o * (z * jax.nn.sigmoid(z))).astype(BF16)


def _attention(proj, cos, sin_a, sin_b, q_gain, k_gain, *, B, S, tq):
    T = B * S
    nq = S // tq
    gw = GROUP * HEAD_DIM
    return pl.pallas_call(
        functools.partial(_attn_kernel, tq=tq),
        grid=(B, ATTN_KV_HEADS, nq),
        in_specs=[
            pl.BlockSpec((tq, gw), lambda b, kv, qi: (b * nq + qi, OFF_AQ // gw + kv)),
            pl.BlockSpec((S, HEAD_DIM), lambda b, kv, qi: (b, OFF_AK // HEAD_DIM + kv)),
            pl.BlockSpec((S, HEAD_DIM), lambda b, kv, qi: (b, OFF_AV // HEAD_DIM + kv)),
            pl.BlockSpec((tq, gw), lambda b, kv, qi: (b * nq + qi, OFF_AZ // gw + kv)),
            pl.BlockSpec((S, HEAD_DIM), lambda b, kv, qi: (0, 0)),
            pl.BlockSpec((S, HEAD_DIM), lambda b, kv, qi: (0, 0)),
            pl.BlockSpec((S, HEAD_DIM), lambda b, kv, qi: (0, 0)),
            pl.BlockSpec((1, HEAD_DIM), lambda b, kv, qi: (0, 0)),
            pl.BlockSpec((1, HEAD_DIM), lambda b, kv, qi: (0, 0)),
        ],
        out_specs=pl.BlockSpec((tq, gw), lambda b, kv, qi: (b * nq + qi, kv)),
        out_shape=jax.ShapeDtypeStruct((T, ATTN_WIDTH), BF16),
        scratch_shapes=[pltpu.VMEM((S, HEAD_DIM), BF16)],
        compiler_params=_cparams(("parallel", "parallel", "arbitrary")),
        name="attention",
    )(proj, proj, proj, proj, cos, sin_a, sin_b, q_gain, k_gain)


def _split3(x):
    hi = x.astype(BF16)
    r1 = x - hi.astype(F32)
    mid = r1.astype(BF16)
    lo = (r1 - mid.astype(F32)).astype(BF16)
    return hi, mid, lo


def _log_sigmoid(x):
    return jnp.minimum(x, 0.0) - jnp.log1p(jnp.exp(-jnp.abs(x)))


def _mlstm_kernel(q_ref, k_ref, v_ref, o_ref, z_ref, gr_ref, gb_ref, ng_ref, out_ref,
                  kt_ref, bf_ref, bb_ref, if_ref, ib_ref, hf_ref, hb_ref,
                  cf_ref, cb_ref, nf_ref, nb_ref, *, nc):
    L = CHUNK
    hd = pl.program_id(1)
    scale = MLSTM_QK_DIM ** -0.5

    t_idx = lax.broadcasted_iota(jnp.int32, (L, L), 0)
    s_idx = lax.broadcasted_iota(jnp.int32, (L, L), 1)
    lower = s_idx <= t_idx
    upper = s_idx >= t_idx

    def gate_rows(col):
        r0 = pl.multiple_of((col * MLSTM_HEADS + hd) * nc, nc)
        return gr_ref[0, pl.ds(r0, nc), :] + gb_ref[pl.ds(r0, nc), :]

    if_ref[...] = gate_rows(0)
    ib_ref[...] = gate_rows(2)
    lf_f = _log_sigmoid(gate_rows(1))
    lf_b = _log_sigmoid(gate_rows(3))
    tri_f = jnp.where(upper, 1.0, 0.0).astype(BF16)
    tri_b = jnp.where(lower, 1.0, 0.0).astype(BF16)
    bf_ref[...] = sum(jnp.dot(p, tri_f, preferred_element_type=F32) for p in _split3(lf_f))
    bb_ref[...] = sum(jnp.dot(p, tri_b, preferred_element_type=F32) for p in _split3(lf_b))

    def tr_body(c, carry):
        rows = pl.ds(pl.multiple_of(c * L, L), L)
        kt_ref[:, rows] = jnp.transpose(k_ref[rows, :].astype(F32)).astype(BF16)
        return carry
    lax.fori_loop(0, nc, tr_body, 0)

    cf_ref[...] = jnp.zeros_like(cf_ref)
    cb_ref[...] = jnp.zeros_like(cb_ref)
    nf_ref[...] = jnp.zeros_like(nf_ref)
    nb_ref[...] = jnp.zeros_like(nb_ref)

    def chunk_step(c, m, b_ref, i_ref, c_ref, n_ref, h_ref, mask, g_lane):
        rows = pl.ds(pl.multiple_of(c * L, L), L)
        brow = b_ref[pl.ds(c, 1), :]
        irow = i_ref[pl.ds(c, 1), :]
        bcol = jnp.transpose(jnp.broadcast_to(brow, (L, L)))
        bcol1 = bcol[:, 0:1]
        g = brow[:, g_lane:g_lane + 1]
        a = jnp.where(mask, (bcol - brow) + irow, -jnp.inf)
        inter = bcol1 + m
        m_t = jnp.maximum(inter, jnp.max(a, axis=-1, keepdims=True))
        w = jnp.exp(a - m_t)
        si = jnp.exp(inter - m_t) * scale
        qc = q_ref[rows, :]
        kt = kt_ref[:, rows]
        vc = v_ref[rows, :]
        qk = jnp.dot(qc, kt, preferred_element_type=F32) * (w * scale)
        c_old = c_ref[...]
        n_old = n_ref[...]
        num = (jnp.dot(qk.astype(BF16), vc, preferred_element_type=F32)
               + si * jnp.dot(qc, c_old.astype(BF16), preferred_element_type=F32))
        qn = jnp.sum(qc.astype(F32) * n_old[0:1, :], axis=-1, keepdims=True)
        den = jnp.sum(qk, axis=-1, keepdims=True) + si * qn
        h_ref[rows, :] = num / jnp.maximum(jnp.abs(den), jnp.exp(-m_t))
        e = (g - brow) + irow
        m_new = jnp.maximum(g + m, jnp.max(e, axis=-1, keepdims=True))
        sc = jnp.exp(g + m - m_new)
        we = jnp.exp(e - m_new)
        kts = (kt.astype(F32) * we).astype(BF16)
        c_ref[...] = sc * c_old + jnp.dot(kts, vc, preferred_element_type=F32)
        we8 = jnp.broadcast_to(we, (8, L)).astype(BF16)
        n_ref[...] = sc * n_old + jnp.dot(we8, k_ref[rows, :], preferred_element_type=F32)
        return m_new

    def body(c, carry):
        m_f, m_b = carry
        m_f = chunk_step(c, m_f, bf_ref, if_ref, cf_ref, nf_ref, hf_ref, lower, L - 1)
        m_b = chunk_step(nc - 1 - c, m_b, bb_ref, ib_ref, cb_ref, nb_ref, hb_ref, upper, 0)
        return m_f, m_b

    zero = jnp.zeros((1, 1), F32)
    lax.fori_loop(0, nc, body, (zero, zero))

    def fin_body(c, carry):
        rows = pl.ds(pl.multiple_of(c * L, L), L)
        hm = (hf_ref[rows, :] + hb_ref[rows, :]) * jax.nn.sigmoid(o_ref[rows, :].astype(F32))
        ms = jnp.mean(hm * hm, axis=-1, keepdims=True)
        hn = hm * lax.rsqrt(ms + EPS) * ng_ref[...]
        z = z_ref[rows, :].astype(F32)
        out_ref[rows, :] = (hn * (z * jax.nn.sigmoid(z))).astype(BF16)
        return carry
    lax.fori_loop(0, nc, fin_body, 0)


def _mlstm(proj, gates_rows, gate_bias_rows, norm_gain, *, B, S):
    T = B * S
    nc = S // CHUNK
    dk, dv = MLSTM_QK_DIM, MLSTM_V_DIM
    n_rows = N_GATE_COLS * nc
    return pl.pallas_call(
        functools.partial(_mlstm_kernel, nc=nc),
        grid=(B, MLSTM_HEADS),
        in_specs=[
            pl.BlockSpec((S, dk), lambda b, h: (b, OFF_MQ // dk + h)),
            pl.BlockSpec((S, dk), lambda b, h: (b, OFF_MK // dk + h)),
            pl.BlockSpec((S, dv), lambda b, h: (b, OFF_MV // dv + h)),
            pl.BlockSpec((S, dv), lambda b, h: (b, OFF_MO // dv + h)),
            pl.BlockSpec((S, dv), lambda b, h: (b, OFF_MZ // dv + h)),
            pl.BlockSpec((1, n_rows, CHUNK), lambda b, h: (b, 0, 0)),
            pl.BlockSpec((n_rows, 1), lambda b, h: (0, 0)),
            pl.BlockSpec((1, dv), lambda b, h: (0, h)),
        ],
        out_specs=pl.BlockSpec((S, dv), lambda b, h: (b, h)),
        out_shape=jax.ShapeDtypeStruct((T, MLSTM_WIDTH), BF16),
        scratch_shapes=[
            pltpu.VMEM((dk, S), BF16),
            pltpu.VMEM((nc, CHUNK), F32),
            pltpu.VMEM((nc, CHUNK), F32),
            pltpu.VMEM((nc, CHUNK), F32),
            pltpu.VMEM((nc, CHUNK), F32),
            pltpu.VMEM((S, dv), F32),
            pltpu.VMEM((S, dv), F32),
            pltpu.VMEM((dk, dv), F32),
            pltpu.VMEM((dk, dv), F32),
            pltpu.VMEM((8, dk), F32),
            pltpu.VMEM((8, dk), F32),
        ],
        compiler_params=_cparams(("parallel", "arbitrary")),
        name="mlstm",
    )(proj, proj, proj, proj, proj, gates_rows, gate_bias_rows, norm_gain)


def _outproj_kernel(a_ref, m_ref, wa_ref, wm_ref, x_ref, y_ref):
    y = jnp.dot(a_ref[...], wa_ref[...], preferred_element_type=F32)
    y = y + jnp.dot(m_ref[...], wm_ref[...], preferred_element_type=F32)
    y_ref[...] = x_ref[...] + y


def _out_projection(attn, mlstm, w_out, x2, *, tm):
    T, D = x2.shape
    return pl.pallas_call(
        _outproj_kernel,
        grid=(T // tm,),
        in_specs=[
            pl.BlockSpec((tm, ATTN_WIDTH), lambda i: (i, 0)),
            pl.BlockSpec((tm, MLSTM_WIDTH), lambda i: (i, 0)),
            pl.BlockSpec((ATTN_WIDTH, D), lambda i: (0, 0)),
            pl.BlockSpec((MLSTM_WIDTH, D), lambda i: (1, 0)),
            pl.BlockSpec((tm, D), lambda i: (i, 0)),
        ],
        out_specs=pl.BlockSpec((tm, D), lambda i: (i, 0)),
        out_shape=jax.ShapeDtypeStruct((T, D), F32),
        compiler_params=_cparams(("parallel",)),
        name="outproj",
    )(attn, mlstm, w_out, w_out, x2)


def _rope_tables(S):
    row = (jnp.arange(S) // GRID_W).astype(F32)
    col = (jnp.arange(S) % GRID_W).astype(F32)
    nf = HEAD_DIM // 4
    inv = 1.0 / (ROPE_THETA ** (jnp.arange(nf, dtype=F32) / nf))
    ang_r = row[:, None] * inv
    ang_c = col[:, None] * inv
    ang = jnp.concatenate([ang_r, ang_r, ang_c, ang_c], axis=-1)
    cos, sin = jnp.cos(ang), jnp.sin(ang)
    quarter = (jnp.arange(HEAD_DIM) // nf) % 2
    sin_a = jnp.where(quarter == 0, -sin, 0.0)
    sin_b = jnp.where(quarter == 1, sin, 0.0)
    return cos, sin_a, sin_b


def _layer(x, norm_g, w_in, b_gates, q_norm_g, k_norm_g, mlstm_norm_g, w_out):
    B, S, D = x.shape
    T = B * S
    nc = S // CHUNK
    x2 = x.reshape(T, D)

    w_main = w_in[:, :N_MAIN].astype(BF16)
    w_gates = jnp.pad(w_in[:, N_MAIN:], ((0, 0), (0, LANES - N_GATE_COLS))).astype(BF16)
    tm = math.gcd(T, 1024)
    proj, gates = _in_projection(x2, norm_g.reshape(1, D), w_main, w_gates, tm=tm, tn=512)

    cos, sin_a, sin_b = _rope_tables(S)
    attn = _attention(proj, cos, sin_a, sin_b, q_norm_g.reshape(1, HEAD_DIM),
                      k_norm_g.reshape(1, HEAD_DIM), B=B, S=S, tq=math.gcd(S, 256))

    gates_rows = gates[:, :N_GATE_COLS].reshape(B, nc, CHUNK, N_GATE_COLS)
    gates_rows = gates_rows.transpose(0, 3, 1, 2).reshape(B, N_GATE_COLS * nc, CHUNK)
    gate_bias_rows = jnp.repeat(b_gates.astype(F32), nc).reshape(N_GATE_COLS * nc, 1)
    mlstm = _mlstm(proj, gates_rows, gate_bias_rows, mlstm_norm_g.reshape(1, MLSTM_WIDTH), B=B, S=S)

    y = _out_projection(attn, mlstm, w_out.astype(BF16), x2, tm=math.gcd(T, 512))
    return y.reshape(B, S, D)


def kernel(x_prompt, x_sample, norm_g, w_in, b_gates, q_norm_g, k_norm_g, mlstm_norm_g, w_out):
    depth = norm_g.shape[0]
    outs = []
    for x in (x_prompt, x_sample):
        for l in range(depth):
            x = _layer(x, norm_g[l], w_in[l], b_gates[l], q_norm_g[l], k_norm_g[l],
                       mlstm_norm_g[l], w_out[l])
        outs.append(x)
    return tuple(outs)
```

```python
import functools
import math

import jax
import jax.numpy as jnp
from jax import lax
from jax.experimental import pallas as pl
from jax.experimental.pallas import tpu as pltpu

F32 = jnp.float32
BF16 = jnp.bfloat16

ATTN_HEADS = 8
ATTN_KV_HEADS = 2
HEAD_DIM = 128
ATTN_WIDTH = ATTN_HEADS * HEAD_DIM
KV_WIDTH = ATTN_KV_HEADS * HEAD_DIM
GROUP = ATTN_HEADS // ATTN_KV_HEADS
MLSTM_HEADS = 4
MLSTM_QK_DIM = 128
MLSTM_V_DIM = 256
MLSTM_QK_WIDTH = MLSTM_HEADS * MLSTM_QK_DIM
MLSTM_WIDTH = MLSTM_HEADS * MLSTM_V_DIM
N_GATE_COLS = 4 * MLSTM_HEADS
GRID_W = 64
ROPE_THETA = 10000.0
CHUNK = 128
EPS = 1e-6

OFF_AQ = 0
OFF_AK = OFF_AQ + ATTN_WIDTH
OFF_AV = OFF_AK + KV_WIDTH
OFF_AZ = OFF_AV + KV_WIDTH
OFF_MQ = OFF_AZ + ATTN_WIDTH
OFF_MK = OFF_MQ + MLSTM_QK_WIDTH
OFF_MV = OFF_MK + MLSTM_QK_WIDTH
OFF_MO = OFF_MV + MLSTM_WIDTH
OFF_MZ = OFF_MO + MLSTM_WIDTH
OFF_GATES = OFF_MZ + MLSTM_WIDTH
N_MAIN = OFF_GATES

LANES = 128
VMEM_LIMIT = 56 * 1024 * 1024


def _cparams(sem):
    return pltpu.CompilerParams(dimension_semantics=sem, vmem_limit_bytes=VMEM_LIMIT)


def _inproj_kernel(x_ref, g_ref, w_ref, wg_ref, bg_ref, o_ref, og_ref, h_ref, *, row_chunk):
    j = pl.program_id(1)
    tm = x_ref.shape[0]

    @pl.when(j == 0)
    def _():
        def body(r, carry):
            rows = pl.ds(pl.multiple_of(r * row_chunk, row_chunk), row_chunk)
            x = x_ref[rows, :]
            ms = jnp.mean(x * x, axis=-1, keepdims=True)
            h_ref[rows, :] = (x * lax.rsqrt(ms + EPS) * g_ref[...]).astype(BF16)
            return carry
        lax.fori_loop(0, tm // row_chunk, body, 0)
        og_ref[...] = jnp.dot(h_ref[...], wg_ref[...], preferred_element_type=F32) + bg_ref[...]

    o_ref[...] = jnp.dot(h_ref[...], w_ref[...], preferred_element_type=F32).astype(BF16)


def _in_projection(x2, norm_g, w_main, w_gates, b_gates, *, tm, tn):
    T, D = x2.shape
    N = w_main.shape[1]
    return pl.pallas_call(
        functools.partial(_inproj_kernel, row_chunk=64),
        grid=(T // tm, N // tn),
        in_specs=[
            pl.BlockSpec((tm, D), lambda i, j: (i, 0)),
            pl.BlockSpec((1, D), lambda i, j: (0, 0)),
            pl.BlockSpec((D, tn), lambda i, j: (0, j)),
            pl.BlockSpec((D, LANES), lambda i, j: (0, 0)),
            pl.BlockSpec((1, LANES), lambda i, j: (0, 0)),
        ],
        out_specs=[
            pl.BlockSpec((tm, tn), lambda i, j: (i, j)),
            pl.BlockSpec((tm, LANES), lambda i, j: (i, 0)),
        ],
        out_shape=[
            jax.ShapeDtypeStruct((T, N), BF16),
            jax.ShapeDtypeStruct((T, LANES), F32),
        ],
        scratch_shapes=[pltpu.VMEM((tm, D), BF16)],
        compiler_params=_cparams(("parallel", "arbitrary")),
        name="inproj",
    )(x2, norm_g, w_main, w_gates, b_gates)


_NT = (((1,), (1,)), ((), ()))


def _norm_rope(x, gain, cos, sin_a, sin_b):
    ms = jnp.mean(x * x, axis=-1, keepdims=True)
    xn = x * lax.rsqrt(ms + EPS) * gain
    up = pltpu.roll(xn, 3 * HEAD_DIM // 4, axis=1)
    down = pltpu.roll(xn, HEAD_DIM // 4, axis=1)
    return xn * cos + up * sin_a + down * sin_b


def _attn_kernel(q_ref, k_ref, v_ref, z_ref, cos_ref, sa_ref, sb_ref, qg_ref, kg_ref,
                 o_ref, kn_ref, vx_ref, *, tq):
    qi = pl.program_id(2)

    @pl.when(qi == 0)
    def _():
        k = k_ref[...].astype(F32)
        kn_ref[...] = _norm_rope(k, kg_ref[...], cos_ref[...], sa_ref[...], sb_ref[...]).astype(BF16)
        vx_ref[:, :HEAD_DIM] = v_ref[...]
        vx_ref[:, HEAD_DIM:] = jnp.ones(v_ref.shape, BF16)

    rows = pl.ds(pl.multiple_of(qi * tq, tq), tq)
    cos = cos_ref[rows, :]
    sa = sa_ref[rows, :]
    sb = sb_ref[rows, :]
    qscale = math.log2(math.e) / math.sqrt(HEAD_DIM)
    def scores(g):
        q = _norm_rope(q_ref[:, g * HEAD_DIM:(g + 1) * HEAD_DIM].astype(F32), qg_ref[...], cos, sa, sb) * qscale
        return lax.dot_general(q.astype(BF16), kn_ref[...], _NT, preferred_element_type=F32)

    s_next = scores(0)
    for g in range(GROUP):
        cols = slice(g * HEAD_DIM, (g + 1) * HEAD_DIM)
        s = s_next
        if g + 1 < GROUP:
            s_next = scores(g + 1)
        m = jnp.max(s, axis=-1, keepdims=True)
        p = jnp.exp2(s - m).astype(BF16)
        ox = jnp.dot(p, vx_ref[...], preferred_element_type=F32)
        o = ox[:, :HEAD_DIM] / ox[:, HEAD_DIM:]
        z = z_ref[:, cols].astype(F32)
        o_ref[:, cols] = (o * (z * jax.nn.sigmoid(z))).astype(BF16)


def _attention(proj, cos, sin_a, sin_b, q_gain, k_gain, *, B, S, tq):
    T = B * S
    nq = S // tq
    gw = GROUP * HEAD_DIM
    return pl.pallas_call(
        functools.partial(_attn_kernel, tq=tq),
        grid=(B, ATTN_KV_HEADS, nq),
        in_specs=[
            pl.BlockSpec((tq, gw), lambda b, kv, qi: (b * nq + qi, OFF_AQ // gw + kv)),
            pl.BlockSpec((S, HEAD_DIM), lambda b, kv, qi: (b, OFF_AK // HEAD_DIM + kv)),
            pl.BlockSpec((S, HEAD_DIM), lambda b, kv, qi: (b, OFF_AV // HEAD_DIM + kv)),
            pl.BlockSpec((tq, gw), lambda b, kv, qi: (b * nq + qi, OFF_AZ // gw + kv)),
            pl.BlockSpec((S, HEAD_DIM), lambda b, kv, qi: (0, 0)),
            pl.BlockSpec((S, HEAD_DIM), lambda b, kv, qi: (0, 0)),
            pl.BlockSpec((S, HEAD_DIM), lambda b, kv, qi: (0, 0)),
            pl.BlockSpec((1, HEAD_DIM), lambda b, kv, qi: (0, 0)),
            pl.BlockSpec((1, HEAD_DIM), lambda b, kv, qi: (0, 0)),
        ],
        out_specs=pl.BlockSpec((tq, gw), lambda b, kv, qi: (b * nq + qi, kv)),
        out_shape=jax.ShapeDtypeStruct((T, ATTN_WIDTH), BF16),
        scratch_shapes=[pltpu.VMEM((S, HEAD_DIM), BF16), pltpu.VMEM((S, 2 * HEAD_DIM), BF16)],
        compiler_params=_cparams(("parallel", "parallel", "arbitrary")),
        name="attention",
    )(proj, proj, proj, proj, cos, sin_a, sin_b, q_gain, k_gain)


def _split3(x):
    hi = x.astype(BF16)
    r1 = x - hi.astype(F32)
    mid = r1.astype(BF16)
    lo = (r1 - mid.astype(F32)).astype(BF16)
    return hi, mid, lo


def _log_sigmoid(x):
    return jnp.minimum(x, 0.0) - jnp.log1p(jnp.exp(-jnp.abs(x)))


def _cummax_lanes(x, reverse):
    n = x.shape[-1]
    lane = lax.broadcasted_iota(jnp.int32, x.shape, 1)
    k = 1
    while k < n:
        if reverse:
            shifted = jnp.where(lane < n - k, pltpu.roll(x, n - k, axis=1), -jnp.inf)
        else:
            shifted = jnp.where(lane >= k, pltpu.roll(x, k, axis=1), -jnp.inf)
        x = jnp.maximum(x, shifted)
        k *= 2
    return x


def _to_columns(row, n):
    return jnp.transpose(jnp.broadcast_to(row, (n, n)))


FWD, BWD = 0, 1
ROW_B, ROW_D, ROW_CM, ROW_WE, ROW_G, ROW_ML = range(6)
N_ROW_KINDS = 6


def _gate_rows_kernel(g_ref, out_ref):
    L = CHUNK
    n = g_ref.shape[1]
    t_idx = lax.broadcasted_iota(jnp.int32, (L, L), 0)
    s_idx = lax.broadcasted_iota(jnp.int32, (L, L), 1)
    for dr in (FWD, BWD):
        i_rows = g_ref[2 * dr]
        lf = _log_sigmoid(g_ref[2 * dr + 1])
        tri = jnp.where(t_idx >= s_idx if dr == BWD else t_idx <= s_idx, 1.0, 0.0).astype(BF16)
        b = sum(jnp.dot(p, tri, preferred_element_type=F32) for p in _split3(lf))
        d = i_rows - b
        g = b[:, 0:1] if dr == BWD else b[:, L - 1:L]
        e = g + d
        mloc = jnp.max(e, axis=-1, keepdims=True)
        out_ref[dr, ROW_B] = b
        out_ref[dr, ROW_D] = d
        out_ref[dr, ROW_CM] = _cummax_lanes(d, reverse=(dr == BWD))
        out_ref[dr, ROW_WE] = jnp.exp(e - mloc)
        out_ref[dr, ROW_G] = jnp.broadcast_to(g, (n, L))
        out_ref[dr, ROW_ML] = jnp.broadcast_to(mloc, (n, L))


def _gate_rows(gates_rows):
    n_rows = gates_rows.shape[1]
    rb = math.gcd(n_rows, 512)
    return pl.pallas_call(
        _gate_rows_kernel,
        grid=(n_rows // rb,),
        in_specs=[pl.BlockSpec((4, rb, CHUNK), lambda i: (0, i, 0))],
        out_specs=pl.BlockSpec((2, N_ROW_KINDS, rb, CHUNK), lambda i: (0, 0, i, 0)),
        out_shape=jax.ShapeDtypeStruct((2, N_ROW_KINDS, n_rows, CHUNK), F32),
        compiler_params=_cparams(("parallel",)),
        name="gate_rows",
    )(gates_rows)


def _mlstm_kernel(q_ref, k_ref, v_ref, o_ref, z_ref, row_ref, ng_ref, out_ref,
                  kt_ref, h_ref, den_ref, u_ref, nu_ref, c_ref, n_ref, *, nc):
    L = CHUNK
    scale = MLSTM_QK_DIM ** -0.5
    B_, D_, CM_, WE_, G_, ML_ = ROW_B, ROW_D, ROW_CM, ROW_WE, ROW_G, ROW_ML

    t_idx = lax.broadcasted_iota(jnp.int32, (L, L), 0)
    s_idx = lax.broadcasted_iota(jnp.int32, (L, L), 1)
    lower = s_idx <= t_idx
    upper = s_idx >= t_idx

    def tr_body(c, carry):
        rows = pl.ds(pl.multiple_of(c * L, L), L)
        kt_ref[:, rows] = jnp.transpose(k_ref[rows, :].astype(F32)).astype(BF16)
        return carry
    lax.fori_loop(0, nc, tr_body, 0, unroll=4)

    ones8 = jnp.ones((8, L), BF16)

    def intra_group(gi, carry):
        cs = [gi * group + j for j in range(group)]
        rows = [pl.ds(pl.multiple_of(c * L, L), L) for c in cs]
        kts = [kt_ref[:, r] for r in rows]
        scores = [jnp.dot(q_ref[r, :], kt, preferred_element_type=F32) for r, kt in zip(rows, kts)]
        for c, r, kt in zip(cs, rows, kts):
            ktf = kt.astype(F32)
            for dr in (FWD, BWD):
                we = row_ref[dr, WE_, pl.ds(c, 1), :]
                u_ref[dr, c] = jnp.dot((ktf * we).astype(BF16), v_ref[r, :], preferred_element_type=F32)
                we8 = jnp.broadcast_to(we, (8, L)).astype(BF16)
                nu_ref[dr, c] = jnp.dot(we8, k_ref[r, :], preferred_element_type=F32)
        qks = []
        for c, s in zip(cs, scores):
            for dr in (FWD, BWD):
                drow = row_ref[dr, D_, pl.ds(c, 1), :]
                cm_col = _to_columns(row_ref[dr, CM_, pl.ds(c, 1), :], L)
                w = jnp.exp(jnp.where(upper if dr == BWD else lower, drow - cm_col, -jnp.inf))
                qks.append((s * w).astype(BF16))
        for j, (c, r) in enumerate(zip(cs, rows)):
            for dr in (FWD, BWD):
                qk = qks[2 * j + dr]
                h_ref[dr, r, :] = jnp.dot(qk, v_ref[r, :], preferred_element_type=F32)
                den_ref[dr, c] = lax.dot_general(ones8, qk, _NT, preferred_element_type=F32)
        return carry
    group = math.gcd(nc, 4)
    lax.fori_loop(0, nc // group, intra_group, 0)

    c_ref[...] = jnp.zeros_like(c_ref)
    n_ref[...] = jnp.zeros_like(n_ref)

    def inter_group(gi, carry):
        m = list(carry)
        chains = []
        for j in range(pair):
            for dr in (FWD, BWD):
                c = gi * pair + j if dr == FWD else nc - 1 - (gi * pair + j)
                brow = row_ref[dr, B_, pl.ds(c, 1), :]
                inter_ = brow + m[dr]
                amax = brow + row_ref[dr, CM_, pl.ds(c, 1), :]
                m_t = jnp.maximum(inter_, amax)
                al = jnp.exp(amax - m_t) * scale
                be = jnp.exp(inter_ - m_t) * scale
                floor = jnp.exp(-m_t)
                g = row_ref[dr, G_, pl.ds(c, 1), :]
                ml = row_ref[dr, ML_, pl.ds(c, 1), :]
                m_new = jnp.maximum(g + m[dr], ml)
                sc = jnp.exp(g + m[dr] - m_new)
                su = jnp.exp(ml - m_new)
                c_old = c_ref[dr]
                n_old = n_ref[dr]
                c_ref[dr] = (jnp.concatenate([sc, sc], axis=1) * c_old
                             + jnp.concatenate([su, su], axis=1) * u_ref[dr, c])
                n_ref[dr] = sc * n_old + su * nu_ref[dr, c]
                m[dr] = m_new
                chains.append((dr, c, al, be, floor, c_old.astype(BF16), n_old.astype(BF16)))
        prods = []
        for dr, c, al, be, floor, c_bf, n_bf in chains:
            qc = q_ref[pl.ds(pl.multiple_of(c * L, L), L), :]
            qn = lax.dot_general(n_bf, qc, _NT, preferred_element_type=F32)[0:1]
            prods.append((qn, jnp.dot(qc, c_bf, preferred_element_type=F32)))
        for (dr, c, al, be, floor, _, _), (qn, q_c) in zip(chains, prods):
            rows = pl.ds(pl.multiple_of(c * L, L), L)
            den = al * den_ref[dr, c][0:1] + be * qn
            r = 1.0 / jnp.maximum(jnp.abs(den), floor)
            ar = _to_columns(al * r, L)
            br = _to_columns(be * r, L)
            h_ref[dr, rows, :] = (jnp.concatenate([ar, ar], axis=1) * h_ref[dr, rows, :]
                                  + jnp.concatenate([br, br], axis=1) * q_c)
        return tuple(m)

    pair = math.gcd(nc, 4)
    zero = jnp.zeros((1, L), F32)
    lax.fori_loop(0, nc // pair, inter_group, (zero, zero))

    def fin_body(c, carry):
        rows = pl.ds(pl.multiple_of(c * L, L), L)
        hm = (h_ref[FWD, rows, :] + h_ref[BWD, rows, :]) * jax.nn.sigmoid(o_ref[rows, :].astype(F32))
        ms = jnp.mean(hm * hm, axis=-1, keepdims=True)
        hn = hm * lax.rsqrt(ms + EPS) * ng_ref[...]
        z = z_ref[rows, :].astype(F32)
        out_ref[rows, :] = (hn * (z * jax.nn.sigmoid(z))).astype(BF16)
        return carry
    lax.fori_loop(0, nc, fin_body, 0, unroll=2)


def _mlstm(proj, rows, norm_gain, *, B, S):
    T = B * S
    nc = S // CHUNK
    dk, dv = MLSTM_QK_DIM, MLSTM_V_DIM
    return pl.pallas_call(
        functools.partial(_mlstm_kernel, nc=nc),
        grid=(B, MLSTM_HEADS),
        in_specs=[
            pl.BlockSpec((S, dk), lambda b, h: (b, OFF_MQ // dk + h)),
            pl.BlockSpec((S, dk), lambda b, h: (b, OFF_MK // dk + h)),
            pl.BlockSpec((S, dv), lambda b, h: (b, OFF_MV // dv + h)),
            pl.BlockSpec((S, dv), lambda b, h: (b, OFF_MO // dv + h)),
            pl.BlockSpec((S, dv), lambda b, h: (b, OFF_MZ // dv + h)),
            pl.BlockSpec((2, N_ROW_KINDS, nc, CHUNK), lambda b, h: (0, 0, b * MLSTM_HEADS + h, 0)),
            pl.BlockSpec((1, dv), lambda b, h: (0, h)),
        ],
        out_specs=pl.BlockSpec((S, dv), lambda b, h: (b, h)),
        out_shape=jax.ShapeDtypeStruct((T, MLSTM_WIDTH), BF16),
        scratch_shapes=[
            pltpu.VMEM((dk, S), BF16),
            pltpu.VMEM((2, S, dv), F32),
            pltpu.VMEM((2, nc, 8, CHUNK), F32),
            pltpu.VMEM((2, nc, dk, dv), F32),
            pltpu.VMEM((2, nc, 8, dk), F32),
            pltpu.VMEM((2, dk, dv), F32),
            pltpu.VMEM((2, 8, dk), F32),
        ],
        compiler_params=_cparams(("parallel", "arbitrary")),
        name="mlstm",
    )(proj, proj, proj, proj, proj, rows, norm_gain)


def _outproj_kernel(a_ref, m_ref, wa_ref, wm_ref, x_ref, y_ref):
    y = jnp.dot(a_ref[...], wa_ref[...], preferred_element_type=F32)
    y = y + jnp.dot(m_ref[...], wm_ref[...], preferred_element_type=F32)
    y_ref[...] = x_ref[...] + y


def _out_projection(attn, mlstm, w_out, x2, *, tm):
    T, D = x2.shape
    return pl.pallas_call(
        _outproj_kernel,
        grid=(T // tm,),
        in_specs=[
            pl.BlockSpec((tm, ATTN_WIDTH), lambda i: (i, 0)),
            pl.BlockSpec((tm, MLSTM_WIDTH), lambda i: (i, 0)),
            pl.BlockSpec((ATTN_WIDTH, D), lambda i: (0, 0)),
            pl.BlockSpec((MLSTM_WIDTH, D), lambda i: (1, 0)),
            pl.BlockSpec((tm, D), lambda i: (i, 0)),
        ],
        out_specs=pl.BlockSpec((tm, D), lambda i: (i, 0)),
        out_shape=jax.ShapeDtypeStruct((T, D), F32),
        compiler_params=_cparams(("parallel",)),
        name="outproj",
    )(attn, mlstm, w_out, w_out, x2)


def _rope_tables(S):
    row = (jnp.arange(S) // GRID_W).astype(F32)
    col = (jnp.arange(S) % GRID_W).astype(F32)
    nf = HEAD_DIM // 4
    inv = 1.0 / (ROPE_THETA ** (jnp.arange(nf, dtype=F32) / nf))
    ang_r = row[:, None] * inv
    ang_c = col[:, None] * inv
    ang = jnp.concatenate([ang_r, ang_r, ang_c, ang_c], axis=-1)
    cos, sin = jnp.cos(ang), jnp.sin(ang)
    quarter = (jnp.arange(HEAD_DIM) // nf) % 2
    sin_a = jnp.where(quarter == 0, -sin, 0.0)
    sin_b = jnp.where(quarter == 1, sin, 0.0)
    return cos, sin_a, sin_b


def _layer(x, norm_g, w_in, b_gates, q_norm_g, k_norm_g, mlstm_norm_g, w_out):
    B, S, D = x.shape
    T = B * S
    nc = S // CHUNK
    x2 = x.reshape(T, D)

    w_main = w_in[:, :N_MAIN].astype(BF16)
    w_gates = jnp.pad(w_in[:, N_MAIN:], ((0, 0), (0, LANES - N_GATE_COLS))).astype(BF16)
    tm = math.gcd(T, 1024)
    bias = jnp.pad(b_gates.astype(F32), (0, LANES - N_GATE_COLS)).reshape(1, LANES)
    proj, gates = _in_projection(x2, norm_g.reshape(1, D), w_main, w_gates, bias, tm=tm, tn=512)

    cos, sin_a, sin_b = _rope_tables(S)
    attn = _attention(proj, cos, sin_a, sin_b, q_norm_g.reshape(1, HEAD_DIM),
                      k_norm_g.reshape(1, HEAD_DIM), B=B, S=S, tq=math.gcd(S, 256))

    gates_rows = gates[:, :N_GATE_COLS].reshape(B, nc, CHUNK, 4, MLSTM_HEADS)
    gates_rows = gates_rows.transpose(3, 0, 4, 1, 2).reshape(4, B * MLSTM_HEADS * nc, CHUNK)
    mlstm = _mlstm(proj, _gate_rows(gates_rows), mlstm_norm_g.reshape(1, MLSTM_WIDTH), B=B, S=S)

    y = _out_projection(attn, mlstm, w_out.astype(BF16), x2, tm=math.gcd(T, 512))
    return y.reshape(B, S, D)


def kernel(x_prompt, x_sample, norm_g, w_in, b_gates, q_norm_g, k_norm_g, mlstm_norm_g, w_out):
    depth = norm_g.shape[0]
    outs = []
    for x in (x_prompt, x_sample):
        for l in range(depth):
            x = _layer(x, norm_g[l], w_in[l], b_gates[l], q_norm_g[l], k_norm_g[l],
                       mlstm_norm_g[l], w_out[l])
        outs.append(x)
    return tuple(outs)
```

```python
import functools
import math

import jax
import jax.numpy as jnp
from jax import lax
from jax.experimental import pallas as pl
from jax.experimental.pallas import tpu as pltpu

F32 = jnp.float32
BF16 = jnp.bfloat16

ATTN_HEADS = 8
ATTN_KV_HEADS = 2
HEAD_DIM = 128
ATTN_WIDTH = ATTN_HEADS * HEAD_DIM
KV_WIDTH = ATTN_KV_HEADS * HEAD_DIM
GROUP = ATTN_HEADS // ATTN_KV_HEADS
MLSTM_HEADS = 4
MLSTM_QK_DIM = 128
MLSTM_V_DIM = 256
MLSTM_QK_WIDTH = MLSTM_HEADS * MLSTM_QK_DIM
MLSTM_WIDTH = MLSTM_HEADS * MLSTM_V_DIM
N_GATE_COLS = 4 * MLSTM_HEADS
GRID_W = 64
ROPE_THETA = 10000.0
CHUNK = 128
EPS = 1e-6

OFF_AQ = 0
OFF_AK = OFF_AQ + ATTN_WIDTH
OFF_AV = OFF_AK + KV_WIDTH
OFF_AZ = OFF_AV + KV_WIDTH
OFF_MQ = OFF_AZ + ATTN_WIDTH
OFF_MK = OFF_MQ + MLSTM_QK_WIDTH
OFF_MV = OFF_MK + MLSTM_QK_WIDTH
OFF_MO = OFF_MV + MLSTM_WIDTH
OFF_MZ = OFF_MO + MLSTM_WIDTH
OFF_GATES = OFF_MZ + MLSTM_WIDTH
N_MAIN = OFF_GATES

LANES = 128
VMEM_LIMIT = 56 * 1024 * 1024


def _cparams(sem):
    return pltpu.CompilerParams(dimension_semantics=sem, vmem_limit_bytes=VMEM_LIMIT)


def _inproj_kernel(x_ref, g_ref, w_ref, wg_ref, bg_ref, o_ref, og_ref, h_ref, *, tn, row_chunk):
    s = pl.program_id(0)
    tm, _ = x_ref.shape
    n_tiles = w_ref.shape[1] // tn
    n_chunks = tm // row_chunk
    nxt = s % 2
    cur = 1 - nxt

    @pl.when(s == 0)
    def _():
        h_ref[1] = jnp.zeros(h_ref.shape[1:], BF16)

    def normalise(r):
        rows = slice(r * row_chunk, (r + 1) * row_chunk)
        x = x_ref[rows, :]
        ms = jnp.mean(x * x, axis=-1, keepdims=True)
        h_ref[nxt, rows, :] = (x * lax.rsqrt(ms + EPS) * g_ref[...]).astype(BF16)

    for j in range(n_tiles):
        if j < n_chunks:
            normalise(j)
        cols = slice(j * tn, (j + 1) * tn)
        o_ref[:, cols] = jnp.dot(h_ref[cur], w_ref[:, cols], preferred_element_type=F32).astype(BF16)
    for r in range(n_tiles, n_chunks):
        normalise(r)
    og_ref[...] = jnp.dot(h_ref[cur], wg_ref[...], preferred_element_type=F32) + bg_ref[...]


def _in_projection(x2, norm_g, w_main, w_gates, b_gates, *, tm, tn):
    T, D = x2.shape
    N = w_main.shape[1]
    n_blocks = T // tm
    resident = pl.Buffered(1)
    return pl.pallas_call(
        functools.partial(_inproj_kernel, tn=tn, row_chunk=32),
        grid=(n_blocks + 1,),
        in_specs=[
            pl.BlockSpec((tm, D), lambda s: (jnp.minimum(s, n_blocks - 1), 0)),
            pl.BlockSpec((1, D), lambda s: (0, 0)),
            pl.BlockSpec((D, N), lambda s: (0, 0), pipeline_mode=resident),
            pl.BlockSpec((D, LANES), lambda s: (0, 0), pipeline_mode=resident),
            pl.BlockSpec((1, LANES), lambda s: (0, 0)),
        ],
        out_specs=[
            pl.BlockSpec((tm, N), lambda s: (jnp.maximum(s - 1, 0), 0)),
            pl.BlockSpec((tm, LANES), lambda s: (jnp.maximum(s - 1, 0), 0)),
        ],
        out_shape=[
            jax.ShapeDtypeStruct((T, N), BF16),
            jax.ShapeDtypeStruct((T, LANES), F32),
        ],
        scratch_shapes=[pltpu.VMEM((2, tm, D), BF16)],
        compiler_params=_cparams(("arbitrary",)),
        name="inproj",
    )(x2, norm_g, w_main, w_gates, b_gates)


_NT = (((1,), (1,)), ((), ()))


def _norm_rope(x, gain, cos, sin_a, sin_b):
    ms = jnp.mean(x * x, axis=-1, keepdims=True)
    xn = x * lax.rsqrt(ms + EPS) * gain
    up = pltpu.roll(xn, 3 * HEAD_DIM // 4, axis=1)
    down = pltpu.roll(xn, HEAD_DIM // 4, axis=1)
    return xn * cos + up * sin_a + down * sin_b


def _attn_kernel(q_ref, k_ref, v_ref, z_ref, cos_ref, sa_ref, sb_ref, qg_ref, kg_ref,
                 o_ref, kn_ref, vx_ref, *, tq):
    qi = pl.program_id(2)

    @pl.when(qi == 0)
    def _():
        k = k_ref[...].astype(F32)
        kn_ref[...] = _norm_rope(k, kg_ref[...], cos_ref[...], sa_ref[...], sb_ref[...]).astype(BF16)
        vx_ref[:, :HEAD_DIM] = v_ref[...]
        vx_ref[:, HEAD_DIM:] = jnp.ones(v_ref.shape, BF16)

    qscale = math.log2(math.e) / math.sqrt(HEAD_DIM)
    block_rows = q_ref.shape[0]
    units = [(t, g) for t in range(block_rows // tq) for g in range(GROUP)]

    def scores(unit):
        t, g = unit
        pos = pl.ds(pl.multiple_of(qi * block_rows + t * tq, tq), tq)
        q = q_ref[t * tq:(t + 1) * tq, g * HEAD_DIM:(g + 1) * HEAD_DIM].astype(F32)
        q = _norm_rope(q, qg_ref[...], cos_ref[pos, :], sa_ref[pos, :], sb_ref[pos, :]) * qscale
        return lax.dot_general(q.astype(BF16), kn_ref[...], _NT, preferred_element_type=F32)

    s_next = scores(units[0])
    for i, (t, g) in enumerate(units):
        rows = slice(t * tq, (t + 1) * tq)
        cols = slice(g * HEAD_DIM, (g + 1) * HEAD_DIM)
        s = s_next
        if i + 1 < len(units):
            s_next = scores(units[i + 1])
        m = jnp.max(s, axis=-1, keepdims=True)
        p = jnp.exp2(s - m).astype(BF16)
        ox = jnp.dot(p, vx_ref[...], preferred_element_type=F32)
        o = ox[:, :HEAD_DIM] / ox[:, HEAD_DIM:]
        z = z_ref[rows, cols].astype(F32)
        o_ref[rows, cols] = (o * (z * jax.nn.sigmoid(z))).astype(BF16)


def _attention(proj, cos, sin_a, sin_b, q_gain, k_gain, *, B, S, tq, tiles_per_step):
    T = B * S
    bq = tq * tiles_per_step
    nq = S // bq
    gw = GROUP * HEAD_DIM
    return pl.pallas_call(
        functools.partial(_attn_kernel, tq=tq),
        grid=(B, ATTN_KV_HEADS, nq),
        in_specs=[
            pl.BlockSpec((bq, gw), lambda b, kv, qi: (b * nq + qi, OFF_AQ // gw + kv)),
            pl.BlockSpec((S, HEAD_DIM), lambda b, kv, qi: (b, OFF_AK // HEAD_DIM + kv)),
            pl.BlockSpec((S, HEAD_DIM), lambda b, kv, qi: (b, OFF_AV // HEAD_DIM + kv)),
            pl.BlockSpec((bq, gw), lambda b, kv, qi: (b * nq + qi, OFF_AZ // gw + kv)),
            pl.BlockSpec((S, HEAD_DIM), lambda b, kv, qi: (0, 0)),
            pl.BlockSpec((S, HEAD_DIM), lambda b, kv, qi: (0, 0)),
            pl.BlockSpec((S, HEAD_DIM), lambda b, kv, qi: (0, 0)),
            pl.BlockSpec((1, HEAD_DIM), lambda b, kv, qi: (0, 0)),
            pl.BlockSpec((1, HEAD_DIM), lambda b, kv, qi: (0, 0)),
        ],
        out_specs=pl.BlockSpec((bq, gw), lambda b, kv, qi: (b * nq + qi, kv)),
        out_shape=jax.ShapeDtypeStruct((T, ATTN_WIDTH), BF16),
        scratch_shapes=[pltpu.VMEM((S, HEAD_DIM), BF16), pltpu.VMEM((S, 2 * HEAD_DIM), BF16)],
        compiler_params=_cparams(("parallel", "parallel", "arbitrary")),
        name="attention",
    )(proj, proj, proj, proj, cos, sin_a, sin_b, q_gain, k_gain)


def _split3(x):
    hi = x.astype(BF16)
    r1 = x - hi.astype(F32)
    mid = r1.astype(BF16)
    lo = (r1 - mid.astype(F32)).astype(BF16)
    return hi, mid, lo


def _log_sigmoid(x):
    return jnp.minimum(x, 0.0) - jnp.log1p(jnp.exp(-jnp.abs(x)))


def _cummax_lanes(x, reverse):
    n = x.shape[-1]
    lane = lax.broadcasted_iota(jnp.int32, x.shape, 1)
    k = 1
    while k < n:
        if reverse:
            shifted = jnp.where(lane < n - k, pltpu.roll(x, n - k, axis=1), -jnp.inf)
        else:
            shifted = jnp.where(lane >= k, pltpu.roll(x, k, axis=1), -jnp.inf)
        x = jnp.maximum(x, shifted)
        k *= 2
    return x


def _to_columns(row, n):
    return jnp.transpose(jnp.broadcast_to(row, (n, n)))


FWD, BWD = 0, 1
ROW_B, ROW_D, ROW_CM, ROW_WE, ROW_G, ROW_ML = range(6)
N_ROW_KINDS = 6


def _gate_rows_kernel(g_ref, out_ref):
    L = CHUNK
    n = g_ref.shape[1]
    t_idx = lax.broadcasted_iota(jnp.int32, (L, L), 0)
    s_idx = lax.broadcasted_iota(jnp.int32, (L, L), 1)
    for dr in (FWD, BWD):
        i_rows = g_ref[2 * dr]
        lf = _log_sigmoid(g_ref[2 * dr + 1])
        tri = jnp.where(t_idx >= s_idx if dr == BWD else t_idx <= s_idx, 1.0, 0.0).astype(BF16)
        b = sum(jnp.dot(p, tri, preferred_element_type=F32) for p in _split3(lf))
        d = i_rows - b
        g = b[:, 0:1] if dr == BWD else b[:, L - 1:L]
        e = g + d
        mloc = jnp.max(e, axis=-1, keepdims=True)
        out_ref[dr, ROW_B] = b
        out_ref[dr, ROW_D] = d
        out_ref[dr, ROW_CM] = _cummax_lanes(d, reverse=(dr == BWD))
        out_ref[dr, ROW_WE] = jnp.exp(e - mloc)
        out_ref[dr, ROW_G] = jnp.broadcast_to(g, (n, L))
        out_ref[dr, ROW_ML] = jnp.broadcast_to(mloc, (n, L))


def _gate_rows(gates_rows):
    n_rows = gates_rows.shape[1]
    rb = math.gcd(n_rows, 512)
    return pl.pallas_call(
        _gate_rows_kernel,
        grid=(n_rows // rb,),
        in_specs=[pl.BlockSpec((4, rb, CHUNK), lambda i: (0, i, 0))],
        out_specs=pl.BlockSpec((2, N_ROW_KINDS, rb, CHUNK), lambda i: (0, 0, i, 0)),
        out_shape=jax.ShapeDtypeStruct((2, N_ROW_KINDS, n_rows, CHUNK), F32),
        compiler_params=_cparams(("parallel",)),
        name="gate_rows",
    )(gates_rows)


def _mlstm_kernel(q_ref, k_ref, v_ref, o_ref, z_ref, row_ref, ng_ref, out_ref,
                  kt_ref, h_ref, den_ref, u_ref, nu_ref, c_ref, n_ref, *, nc):
    L = CHUNK
    scale = MLSTM_QK_DIM ** -0.5
    B_, D_, CM_, WE_, G_, ML_ = ROW_B, ROW_D, ROW_CM, ROW_WE, ROW_G, ROW_ML

    t_idx = lax.broadcasted_iota(jnp.int32, (L, L), 0)
    s_idx = lax.broadcasted_iota(jnp.int32, (L, L), 1)
    lower = s_idx <= t_idx
    upper = s_idx >= t_idx

    def tr_body(c, carry):
        rows = pl.ds(pl.multiple_of(c * L, L), L)
        kt_ref[:, rows] = jnp.transpose(k_ref[rows, :].astype(F32)).astype(BF16)
        return carry
    lax.fori_loop(0, nc, tr_body, 0, unroll=4)

    ones8 = jnp.ones((8, L), BF16)

    def intra_group(gi, carry):
        cs = [gi * group + j for j in range(group)]
        rows = [pl.ds(pl.multiple_of(c * L, L), L) for c in cs]
        kts = [kt_ref[:, r] for r in rows]
        scores = [jnp.dot(q_ref[r, :], kt, preferred_element_type=F32) for r, kt in zip(rows, kts)]
        for c, r, kt in zip(cs, rows, kts):
            ktf = kt.astype(F32)
            for dr in (FWD, BWD):
                we = row_ref[dr, WE_, pl.ds(c, 1), :]
                u_ref[dr, c] = jnp.dot((ktf * we).astype(BF16), v_ref[r, :], preferred_element_type=F32)
                we8 = jnp.broadcast_to(we, (8, L)).astype(BF16)
                nu_ref[dr, c] = jnp.dot(we8, k_ref[r, :], preferred_element_type=F32)
        qks = []
        for c, s in zip(cs, scores):
            for dr in (FWD, BWD):
                drow = row_ref[dr, D_, pl.ds(c, 1), :]
                cm_col = _to_columns(row_ref[dr, CM_, pl.ds(c, 1), :], L)
                w = jnp.exp(jnp.where(upper if dr == BWD else lower, drow - cm_col, -jnp.inf))
                qks.append((s * w).astype(BF16))
        for j, (c, r) in enumerate(zip(cs, rows)):
            for dr in (FWD, BWD):
                qk = qks[2 * j + dr]
                h_ref[dr, r, :] = jnp.dot(qk, v_ref[r, :], preferred_element_type=F32)
                den_ref[dr, c] = lax.dot_general(ones8, qk, _NT, preferred_element_type=F32)
        return carry
    group = math.gcd(nc, 4)
    lax.fori_loop(0, nc // group, intra_group, 0)

    c_ref[...] = jnp.zeros_like(c_ref)
    n_ref[...] = jnp.zeros_like(n_ref)

    def inter_group(gi, carry):
        m = list(carry)
        chains = []
        for j in range(pair):
            for dr in (FWD, BWD):
                c = gi * pair + j if dr == FWD else nc - 1 - (gi * pair + j)
                brow = row_ref[dr, B_, pl.ds(c, 1), :]
                inter_ = brow + m[dr]
                amax = brow + row_ref[dr, CM_, pl.ds(c, 1), :]
                m_t = jnp.maximum(inter_, amax)
                al = jnp.exp(amax - m_t) * scale
                be = jnp.exp(inter_ - m_t) * scale
                floor = jnp.exp(-m_t)
                g = row_ref[dr, G_, pl.ds(c, 1), :]
                ml = row_ref[dr, ML_, pl.ds(c, 1), :]
                m_new = jnp.maximum(g + m[dr], ml)
                sc = jnp.exp(g + m[dr] - m_new)
                su = jnp.exp(ml - m_new)
                c_old = c_ref[dr]
                n_old = n_ref[dr]
                c_ref[dr] = (jnp.concatenate([sc, sc], axis=1) * c_old
                             + jnp.concatenate([su, su], axis=1) * u_ref[dr, c])
                n_ref[dr] = sc * n_old + su * nu_ref[dr, c]
                m[dr] = m_new
                chains.append((dr, c, al, be, floor, c_old.astype(BF16), n_old.astype(BF16)))
        prods = []
        for dr, c, al, be, floor, c_bf, n_bf in chains:
            qc = q_ref[pl.ds(pl.multiple_of(c * L, L), L), :]
            qn = lax.dot_general(n_bf, qc, _NT, preferred_element_type=F32)[0:1]
            prods.append((qn, jnp.dot(qc, c_bf, preferred_element_type=F32)))
        for (dr, c, al, be, floor, _, _), (qn, q_c) in zip(chains, prods):
            rows = pl.ds(pl.multiple_of(c * L, L), L)
            den = al * den_ref[dr, c][0:1] + be * qn
            r = 1.0 / jnp.maximum(jnp.abs(den), floor)
            ar = _to_columns(al * r, L)
            br = _to_columns(be * r, L)
            h_ref[dr, rows, :] = (jnp.concatenate([ar, ar], axis=1) * h_ref[dr, rows, :]
                                  + jnp.concatenate([br, br], axis=1) * q_c)
        return tuple(m)

    pair = math.gcd(nc, 4)
    zero = jnp.zeros((1, L), F32)
    lax.fori_loop(0, nc // pair, inter_group, (zero, zero))

    def fin_body(c, carry):
        rows = pl.ds(pl.multiple_of(c * L, L), L)
        hm = (h_ref[FWD, rows, :] + h_ref[BWD, rows, :]) * jax.nn.sigmoid(o_ref[rows, :].astype(F32))
        ms = jnp.mean(hm * hm, axis=-1, keepdims=True)
        hn = hm * lax.rsqrt(ms + EPS) * ng_ref[...]
        z = z_ref[rows, :].astype(F32)
        out_ref[rows, :] = (hn * (z * jax.nn.sigmoid(z))).astype(BF16)
        return carry
    lax.fori_loop(0, nc, fin_body, 0, unroll=2)


def _mlstm(proj, rows, norm_gain, *, B, S):
    T = B * S
    nc = S // CHUNK
    dk, dv = MLSTM_QK_DIM, MLSTM_V_DIM
    return pl.pallas_call(
        functools.partial(_mlstm_kernel, nc=nc),
        grid=(B, MLSTM_HEADS),
        in_specs=[
            pl.BlockSpec((S, dk), lambda b, h: (b, OFF_MQ // dk + h)),
            pl.BlockSpec((S, dk), lambda b, h: (b, OFF_MK // dk + h)),
            pl.BlockSpec((S, dv), lambda b, h: (b, OFF_MV // dv + h)),
            pl.BlockSpec((S, dv), lambda b, h: (b, OFF_MO // dv + h)),
            pl.BlockSpec((S, dv), lambda b, h: (b, OFF_MZ // dv + h)),
            pl.BlockSpec((2, N_ROW_KINDS, nc, CHUNK), lambda b, h: (0, 0, b * MLSTM_HEADS + h, 0)),
            pl.BlockSpec((1, dv), lambda b, h: (0, h)),
        ],
        out_specs=pl.BlockSpec((S, dv), lambda b, h: (b, h)),
        out_shape=jax.ShapeDtypeStruct((T, MLSTM_WIDTH), BF16),
        scratch_shapes=[
            pltpu.VMEM((dk, S), BF16),
            pltpu.VMEM((2, S, dv), F32),
            pltpu.VMEM((2, nc, 8, CHUNK), F32),
            pltpu.VMEM((2, nc, dk, dv), F32),
            pltpu.VMEM((2, nc, 8, dk), F32),
            pltpu.VMEM((2, dk, dv), F32),
            pltpu.VMEM((2, 8, dk), F32),
        ],
        compiler_params=_cparams(("parallel", "arbitrary")),
        name="mlstm",
    )(proj, proj, proj, proj, proj, rows, norm_gain)


def _outproj_kernel(a_ref, m_ref, wa_ref, wm_ref, x_ref, y_ref):
    y = jnp.dot(a_ref[...], wa_ref[...], preferred_element_type=F32)
    y = y + jnp.dot(m_ref[...], wm_ref[...], preferred_element_type=F32)
    y_ref[...] = x_ref[...] + y


def _out_projection(attn, mlstm, w_out, x2, *, tm):
    T, D = x2.shape
    return pl.pallas_call(
        _outproj_kernel,
        grid=(T // tm,),
        in_specs=[
            pl.BlockSpec((tm, ATTN_WIDTH), lambda i: (i, 0)),
            pl.BlockSpec((tm, MLSTM_WIDTH), lambda i: (i, 0)),
            pl.BlockSpec((ATTN_WIDTH, D), lambda i: (0, 0)),
            pl.BlockSpec((MLSTM_WIDTH, D), lambda i: (1, 0)),
            pl.BlockSpec((tm, D), lambda i: (i, 0)),
        ],
        out_specs=pl.BlockSpec((tm, D), lambda i: (i, 0)),
        out_shape=jax.ShapeDtypeStruct((T, D), F32),
        compiler_params=_cparams(("parallel",)),
        name="outproj",
    )(attn, mlstm, w_out, w_out, x2)


def _rope_tables(S):
    row = (jnp.arange(S) // GRID_W).astype(F32)
    col = (jnp.arange(S) % GRID_W).astype(F32)
    nf = HEAD_DIM // 4
    inv = 1.0 / (ROPE_THETA ** (jnp.arange(nf, dtype=F32) / nf))
    ang_r = row[:, None] * inv
    ang_c = col[:, None] * inv
    ang = jnp.concatenate([ang_r, ang_r, ang_c, ang_c], axis=-1)
    cos, sin = jnp.cos(ang), jnp.sin(ang)
    quarter = (jnp.arange(HEAD_DIM) // nf) % 2
    sin_a = jnp.where(quarter == 0, -sin, 0.0)
    sin_b = jnp.where(quarter == 1, sin, 0.0)
    return cos, sin_a, sin_b


def _layer(x, norm_g, w_in, b_gates, q_norm_g, k_norm_g, mlstm_norm_g, w_out):
    B, S, D = x.shape
    T = B * S
    nc = S // CHUNK
    x2 = x.reshape(T, D)

    w_main = w_in[:, :N_MAIN].astype(BF16)
    w_gates = jnp.pad(w_in[:, N_MAIN:], ((0, 0), (0, LANES - N_GATE_COLS))).astype(BF16)
    tm = math.gcd(T, 256)
    bias = jnp.pad(b_gates.astype(F32), (0, LANES - N_GATE_COLS)).reshape(1, LANES)
    proj, gates = _in_projection(x2, norm_g.reshape(1, D), w_main, w_gates, bias, tm=tm, tn=512)

    cos, sin_a, sin_b = _rope_tables(S)
    attn = _attention(proj, cos, sin_a, sin_b, q_norm_g.reshape(1, HEAD_DIM),
                      k_norm_g.reshape(1, HEAD_DIM), B=B, S=S, tq=math.gcd(S, 256), tiles_per_step=4 if S % 1024 == 0 else 1)

    gates_rows = gates[:, :N_GATE_COLS].reshape(B, nc, CHUNK, 4, MLSTM_HEADS)
    gates_rows = gates_rows.transpose(3, 0, 4, 1, 2).reshape(4, B * MLSTM_HEADS * nc, CHUNK)
    mlstm = _mlstm(proj, _gate_rows(gates_rows), mlstm_norm_g.reshape(1, MLSTM_WIDTH), B=B, S=S)

    y = _out_projection(attn, mlstm, w_out.astype(BF16), x2, tm=math.gcd(T, 512))
    return y.reshape(B, S, D)


def kernel(x_prompt, x_sample, norm_g, w_in, b_gates, q_norm_g, k_norm_g, mlstm_norm_g, w_out):
    depth = norm_g.shape[0]
    outs = []
    for x in (x_prompt, x_sample):
        for l in range(depth):
            x = _layer(x, norm_g[l], w_in[l], b_gates[l], q_norm_g[l], k_norm_g[l],
                       mlstm_norm_g[l], w_out[l])
        outs.append(x)
    return tuple(outs)
```

```python
import functools
import math

import jax
import jax.numpy as jnp
from jax import lax
from jax.experimental import pallas as pl
from jax.experimental.pallas import tpu as pltpu

F32 = jnp.float32
BF16 = jnp.bfloat16

ATTN_HEADS = 8
ATTN_KV_HEADS = 2
HEAD_DIM = 128
ATTN_WIDTH = ATTN_HEADS * HEAD_DIM
KV_WIDTH = ATTN_KV_HEADS * HEAD_DIM
GROUP = ATTN_HEADS // ATTN_KV_HEADS
MLSTM_HEADS = 4
MLSTM_QK_DIM = 128
MLSTM_V_DIM = 256
MLSTM_QK_WIDTH = MLSTM_HEADS * MLSTM_QK_DIM
MLSTM_WIDTH = MLSTM_HEADS * MLSTM_V_DIM
N_GATE_COLS = 4 * MLSTM_HEADS
GRID_W = 64
ROPE_THETA = 10000.0
CHUNK = 128
EPS = 1e-6

OFF_AQ = 0
OFF_AK = OFF_AQ + ATTN_WIDTH
OFF_AV = OFF_AK + KV_WIDTH
OFF_AZ = OFF_AV + KV_WIDTH
OFF_MQ = OFF_AZ + ATTN_WIDTH
OFF_MK = OFF_MQ + MLSTM_QK_WIDTH
OFF_MV = OFF_MK + MLSTM_QK_WIDTH
OFF_MO = OFF_MV + MLSTM_WIDTH
OFF_MZ = OFF_MO + MLSTM_WIDTH
OFF_GATES = OFF_MZ + MLSTM_WIDTH
N_MAIN = OFF_GATES

LANES = 128
VMEM_LIMIT = 56 * 1024 * 1024


def _cparams(sem):
    return pltpu.CompilerParams(dimension_semantics=sem, vmem_limit_bytes=VMEM_LIMIT)


def _inproj_kernel(x_ref, g_ref, w_ref, wg_ref, bg_ref, o_ref, og_ref, h_ref, *, tn, row_chunk):
    s = pl.program_id(0)
    tm, _ = x_ref.shape
    n_tiles = w_ref.shape[1] // tn
    n_chunks = tm // row_chunk
    nxt = s % 2
    cur = 1 - nxt

    @pl.when(s == 0)
    def _():
        h_ref[1] = jnp.zeros(h_ref.shape[1:], BF16)

    def normalise(r):
        rows = slice(r * row_chunk, (r + 1) * row_chunk)
        x = x_ref[rows, :]
        ms = jnp.mean(x * x, axis=-1, keepdims=True)
        h_ref[nxt, rows, :] = (x * lax.rsqrt(ms + EPS) * g_ref[...]).astype(BF16)

    for j in range(n_tiles):
        if j < n_chunks:
            normalise(j)
        cols = slice(j * tn, (j + 1) * tn)
        o_ref[:, cols] = jnp.dot(h_ref[cur], w_ref[:, cols], preferred_element_type=F32).astype(BF16)
    for r in range(n_tiles, n_chunks):
        normalise(r)
    og_ref[...] = jnp.dot(h_ref[cur], wg_ref[...], preferred_element_type=F32) + bg_ref[...]


def _in_projection(x2, norm_g, w_main, w_gates, b_gates, *, tm, tn):
    T, D = x2.shape
    N = w_main.shape[1]
    n_blocks = T // tm
    resident = pl.Buffered(1)
    return pl.pallas_call(
        functools.partial(_inproj_kernel, tn=tn, row_chunk=32),
        grid=(n_blocks + 1,),
        in_specs=[
            pl.BlockSpec((tm, D), lambda s: (jnp.minimum(s, n_blocks - 1), 0)),
            pl.BlockSpec((1, D), lambda s: (0, 0)),
            pl.BlockSpec((D, N), lambda s: (0, 0), pipeline_mode=resident),
            pl.BlockSpec((D, LANES), lambda s: (0, 0), pipeline_mode=resident),
            pl.BlockSpec((1, LANES), lambda s: (0, 0)),
        ],
        out_specs=[
            pl.BlockSpec((tm, N), lambda s: (jnp.maximum(s - 1, 0), 0)),
            pl.BlockSpec((tm, LANES), lambda s: (jnp.maximum(s - 1, 0), 0)),
        ],
        out_shape=[
            jax.ShapeDtypeStruct((T, N), BF16),
            jax.ShapeDtypeStruct((T, LANES), F32),
        ],
        scratch_shapes=[pltpu.VMEM((2, tm, D), BF16)],
        compiler_params=_cparams(("arbitrary",)),
        name="inproj",
    )(x2, norm_g, w_main, w_gates, b_gates)


_NT = (((1,), (1,)), ((), ()))


def _sigmoid(x):
    return 0.5 * jnp.tanh(0.5 * x) + 0.5


def _silu(x):
    hx = 0.5 * x
    return hx * jnp.tanh(hx) + hx


def _norm_rope(x, gain, cos, sin_a, sin_b):
    ms = jnp.mean(x * x, axis=-1, keepdims=True)
    xn = x * lax.rsqrt(ms + EPS) * gain
    up = pltpu.roll(xn, 3 * HEAD_DIM // 4, axis=1)
    down = pltpu.roll(xn, HEAD_DIM // 4, axis=1)
    return xn * cos + up * sin_a + down * sin_b


def _attn_kernel(q_ref, k_ref, v_ref, z_ref, cos_ref, sa_ref, sb_ref, qg_ref, kg_ref,
                 o_ref, kn_ref, vx_ref, *, tq):
    qi = pl.program_id(2)

    @pl.when(qi == 0)
    def _():
        k = k_ref[...].astype(F32)
        kn_ref[...] = _norm_rope(k, kg_ref[...], cos_ref[...], sa_ref[...], sb_ref[...]).astype(BF16)
        vx_ref[:, :HEAD_DIM] = v_ref[...]
        vx_ref[:, HEAD_DIM:] = jnp.ones(v_ref.shape, BF16)

    qscale = math.log2(math.e) / math.sqrt(HEAD_DIM)
    block_rows = q_ref.shape[0]
    units = [(t, g) for t in range(block_rows // tq) for g in range(GROUP)]

    def scores(unit):
        t, g = unit
        pos = pl.ds(pl.multiple_of(qi * block_rows + t * tq, tq), tq)
        q = q_ref[t * tq:(t + 1) * tq, g * HEAD_DIM:(g + 1) * HEAD_DIM].astype(F32)
        q = _norm_rope(q, qg_ref[...], cos_ref[pos, :], sa_ref[pos, :], sb_ref[pos, :]) * qscale
        return lax.dot_general(q.astype(BF16), kn_ref[...], _NT, preferred_element_type=F32)

    s_next = scores(units[0])
    for i, (t, g) in enumerate(units):
        rows = slice(t * tq, (t + 1) * tq)
        cols = slice(g * HEAD_DIM, (g + 1) * HEAD_DIM)
        s = s_next
        if i + 1 < len(units):
            s_next = scores(units[i + 1])
        m = jnp.max(s, axis=-1, keepdims=True)
        p = jnp.exp2(s - m).astype(BF16)
        ox = jnp.dot(p, vx_ref[...], preferred_element_type=F32)
        o = ox[:, :HEAD_DIM] / ox[:, HEAD_DIM:]
        z = z_ref[rows, cols].astype(F32)
        o_ref[rows, cols] = (o * _silu(z)).astype(BF16)


def _attention(proj, cos, sin_a, sin_b, q_gain, k_gain, *, B, S, tq, tiles_per_step):
    T = B * S
    bq = tq * tiles_per_step
    nq = S // bq
    gw = GROUP * HEAD_DIM
    return pl.pallas_call(
        functools.partial(_attn_kernel, tq=tq),
        grid=(B, ATTN_KV_HEADS, nq),
        in_specs=[
            pl.BlockSpec((bq, gw), lambda b, kv, qi: (b * nq + qi, OFF_AQ // gw + kv)),
            pl.BlockSpec((S, HEAD_DIM), lambda b, kv, qi: (b, OFF_AK // HEAD_DIM + kv)),
            pl.BlockSpec((S, HEAD_DIM), lambda b, kv, qi: (b, OFF_AV // HEAD_DIM + kv)),
            pl.BlockSpec((bq, gw), lambda b, kv, qi: (b * nq + qi, OFF_AZ // gw + kv)),
            pl.BlockSpec((S, HEAD_DIM), lambda b, kv, qi: (0, 0)),
            pl.BlockSpec((S, HEAD_DIM), lambda b, kv, qi: (0, 0)),
            pl.BlockSpec((S, HEAD_DIM), lambda b, kv, qi: (0, 0)),
            pl.BlockSpec((1, HEAD_DIM), lambda b, kv, qi: (0, 0)),
            pl.BlockSpec((1, HEAD_DIM), lambda b, kv, qi: (0, 0)),
        ],
        out_specs=pl.BlockSpec((bq, gw), lambda b, kv, qi: (b * nq + qi, kv)),
        out_shape=jax.ShapeDtypeStruct((T, ATTN_WIDTH), BF16),
        scratch_shapes=[pltpu.VMEM((S, HEAD_DIM), BF16), pltpu.VMEM((S, 2 * HEAD_DIM), BF16)],
        compiler_params=_cparams(("parallel", "parallel", "arbitrary")),
        name="attention",
    )(proj, proj, proj, proj, cos, sin_a, sin_b, q_gain, k_gain)


def _split3(x):
    hi = x.astype(BF16)
    r1 = x - hi.astype(F32)
    mid = r1.astype(BF16)
    lo = (r1 - mid.astype(F32)).astype(BF16)
    return hi, mid, lo


def _log_sigmoid(x):
    return jnp.minimum(x, 0.0) - jnp.log1p(jnp.exp(-jnp.abs(x)))


def _cummax_lanes(x, reverse):
    n = x.shape[-1]
    lane = lax.broadcasted_iota(jnp.int32, x.shape, 1)
    k = 1
    while k < n:
        if reverse:
            shifted = jnp.where(lane < n - k, pltpu.roll(x, n - k, axis=1), -jnp.inf)
        else:
            shifted = jnp.where(lane >= k, pltpu.roll(x, k, axis=1), -jnp.inf)
        x = jnp.maximum(x, shifted)
        k *= 2
    return x


def _to_columns(row, n):
    return jnp.transpose(jnp.broadcast_to(row, (n, n)))


FWD, BWD = 0, 1
ROW_B, ROW_D, ROW_CM, ROW_WE, ROW_G, ROW_ML = range(6)
N_ROW_KINDS = 6


def _gate_rows_kernel(g_ref, out_ref):
    L = CHUNK
    n = g_ref.shape[1]
    t_idx = lax.broadcasted_iota(jnp.int32, (L, L), 0)
    s_idx = lax.broadcasted_iota(jnp.int32, (L, L), 1)
    for dr in (FWD, BWD):
        i_rows = g_ref[2 * dr]
        lf = _log_sigmoid(g_ref[2 * dr + 1])
        tri = jnp.where(t_idx >= s_idx if dr == BWD else t_idx <= s_idx, 1.0, 0.0).astype(BF16)
        b = sum(jnp.dot(p, tri, preferred_element_type=F32) for p in _split3(lf))
        d = i_rows - b
        g = b[:, 0:1] if dr == BWD else b[:, L - 1:L]
        e = g + d
        mloc = jnp.max(e, axis=-1, keepdims=True)
        out_ref[dr, ROW_B] = b
        out_ref[dr, ROW_D] = d
        out_ref[dr, ROW_CM] = _cummax_lanes(d, reverse=(dr == BWD))
        out_ref[dr, ROW_WE] = jnp.exp(e - mloc)
        out_ref[dr, ROW_G] = jnp.broadcast_to(g, (n, L))
        out_ref[dr, ROW_ML] = jnp.broadcast_to(mloc, (n, L))


def _gate_rows(gates_rows):
    n_rows = gates_rows.shape[1]
    rb = math.gcd(n_rows, 512)
    return pl.pallas_call(
        _gate_rows_kernel,
        grid=(n_rows // rb,),
        in_specs=[pl.BlockSpec((4, rb, CHUNK), lambda i: (0, i, 0))],
        out_specs=pl.BlockSpec((2, N_ROW_KINDS, rb, CHUNK), lambda i: (0, 0, i, 0)),
        out_shape=jax.ShapeDtypeStruct((2, N_ROW_KINDS, n_rows, CHUNK), F32),
        compiler_params=_cparams(("parallel",)),
        name="gate_rows",
    )(gates_rows)


def _mlstm_kernel(q_ref, k_ref, v_ref, o_ref, z_ref, row_ref, ng_ref, out_ref,
                  mix_ref, coef_ref, den_ref, u_ref, nu_ref, c_ref, n_ref, *, nc):
    L = CHUNK
    scale = MLSTM_QK_DIM ** -0.5
    B_, D_, CM_, WE_, G_, ML_ = ROW_B, ROW_D, ROW_CM, ROW_WE, ROW_G, ROW_ML

    t_idx = lax.broadcasted_iota(jnp.int32, (L, L), 0)
    s_idx = lax.broadcasted_iota(jnp.int32, (L, L), 1)
    lower = s_idx <= t_idx
    upper = s_idx >= t_idx
    eye = jnp.where(s_idx == t_idx, 1.0, 0.0).astype(BF16)

    ones8 = jnp.ones((8, L), BF16)

    def intra_group(gi, carry):
        cs = [gi * group + j for j in range(group)]
        rows = [pl.ds(pl.multiple_of(c * L, L), L) for c in cs]
        kts = [jnp.transpose(k_ref[r, :].astype(F32)).astype(BF16) for r in rows]
        scores = [jnp.dot(q_ref[r, :], kt, preferred_element_type=F32) for r, kt in zip(rows, kts)]
        for c, r, kt in zip(cs, rows, kts):
            ktf = kt.astype(F32)
            for dr in (FWD, BWD):
                we = row_ref[dr, WE_, pl.ds(c, 1), :]
                u_ref[dr, c] = jnp.dot((ktf * we).astype(BF16), v_ref[r, :], preferred_element_type=F32)
                we8 = jnp.broadcast_to(we, (8, L)).astype(BF16)
                nu_ref[dr, c] = jnp.dot(we8, k_ref[r, :], preferred_element_type=F32)
        qks = []
        for c, s in zip(cs, scores):
            for dr in (FWD, BWD):
                drow = row_ref[dr, D_, pl.ds(c, 1), :]
                cm_col = _to_columns(row_ref[dr, CM_, pl.ds(c, 1), :], L)
                w = jnp.exp(jnp.where(upper if dr == BWD else lower, drow - cm_col, -jnp.inf))
                qks.append((s * w).astype(BF16))
        for j, (c, r) in enumerate(zip(cs, rows)):
            for dr in (FWD, BWD):
                qk = qks[2 * j + dr]
                mix_ref[c, 2 * dr * L:(2 * dr + 1) * L, :] = jnp.dot(
                    qk, v_ref[r, :], preferred_element_type=F32).astype(BF16)
                den_ref[dr, c] = lax.dot_general(ones8, qk, _NT, preferred_element_type=F32)
        return carry
    group = math.gcd(nc, 8)
    lax.fori_loop(0, nc // group, intra_group, 0)

    c_ref[...] = jnp.zeros_like(c_ref)
    n_ref[...] = jnp.zeros_like(n_ref)

    def inter_group(gi, carry):
        m = list(carry)
        chains = []
        for j in range(pair):
            for dr in (FWD, BWD):
                c = gi * pair + j if dr == FWD else nc - 1 - (gi * pair + j)
                brow = row_ref[dr, B_, pl.ds(c, 1), :]
                inter_ = brow + m[dr]
                amax = brow + row_ref[dr, CM_, pl.ds(c, 1), :]
                m_t = jnp.maximum(inter_, amax)
                al = jnp.exp(amax - m_t) * scale
                be = jnp.exp(inter_ - m_t) * scale
                floor = jnp.exp(-m_t)
                g = row_ref[dr, G_, pl.ds(c, 1), :]
                ml = row_ref[dr, ML_, pl.ds(c, 1), :]
                m_new = jnp.maximum(g + m[dr], ml)
                sc = jnp.exp(g + m[dr] - m_new)
                su = jnp.exp(ml - m_new)
                c_old = c_ref[dr]
                n_old = n_ref[dr]
                c_ref[dr] = (jnp.concatenate([sc, sc], axis=1) * c_old
                             + jnp.concatenate([su, su], axis=1) * u_ref[dr, c])
                n_ref[dr] = sc * n_old + su * nu_ref[dr, c]
                m[dr] = m_new
                chains.append((dr, c, al, be, floor, c_old.astype(BF16), n_old.astype(BF16)))
        prods = []
        for dr, c, al, be, floor, c_bf, n_bf in chains:
            qc = q_ref[pl.ds(pl.multiple_of(c * L, L), L), :]
            qn = lax.dot_general(n_bf, qc, _NT, preferred_element_type=F32)[0:1]
            prods.append((qn, jnp.dot(qc, c_bf, preferred_element_type=F32)))
        for (dr, c, al, be, floor, _, _), (qn, q_c) in zip(chains, prods):
            den = al * den_ref[dr, c][0:1] + be * qn
            r = 1.0 / jnp.maximum(jnp.abs(den), floor)
            coef_ref[c, 2 * dr:2 * dr + 1, :] = al * r
            coef_ref[c, 2 * dr + 1:2 * dr + 2, :] = be * r
            mix_ref[c, (2 * dr + 1) * L:(2 * dr + 2) * L, :] = q_c.astype(BF16)
        return tuple(m)

    pair = math.gcd(nc, 8)
    zero = jnp.zeros((1, L), F32)
    lax.fori_loop(0, nc // pair, inter_group, (zero, zero))

    def fin_group(gi, carry):
        cs = [gi * fin + j for j in range(fin)]
        hs = []
        for c in cs:
            diags = [eye * jnp.broadcast_to(coef_ref[c, x:x + 1, :].astype(BF16), (L, L)) for x in range(4)]
            hs.append(jnp.dot(jnp.concatenate(diags, axis=1), mix_ref[c], preferred_element_type=F32))
        for c, h in zip(cs, hs):
            rows = pl.ds(pl.multiple_of(c * L, L), L)
            hm = h * _sigmoid(o_ref[rows, :].astype(F32))
            ms = jnp.mean(hm * hm, axis=-1, keepdims=True)
            hn = hm * lax.rsqrt(ms + EPS) * ng_ref[...]
            z = z_ref[rows, :].astype(F32)
            out_ref[rows, :] = (hn * _silu(z)).astype(BF16)
        return carry
    fin = math.gcd(nc, 4)
    lax.fori_loop(0, nc // fin, fin_group, 0)


def _mlstm(proj, rows, norm_gain, *, B, S):
    T = B * S
    nc = S // CHUNK
    dk, dv = MLSTM_QK_DIM, MLSTM_V_DIM
    return pl.pallas_call(
        functools.partial(_mlstm_kernel, nc=nc),
        grid=(B, MLSTM_HEADS),
        in_specs=[
            pl.BlockSpec((S, dk), lambda b, h: (b, OFF_MQ // dk + h)),
            pl.BlockSpec((S, dk), lambda b, h: (b, OFF_MK // dk + h)),
            pl.BlockSpec((S, dv), lambda b, h: (b, OFF_MV // dv + h)),
            pl.BlockSpec((S, dv), lambda b, h: (b, OFF_MO // dv + h)),
            pl.BlockSpec((S, dv), lambda b, h: (b, OFF_MZ // dv + h)),
            pl.BlockSpec((2, N_ROW_KINDS, nc, CHUNK), lambda b, h: (0, 0, b * MLSTM_HEADS + h, 0)),
            pl.BlockSpec((1, dv), lambda b, h: (0, h)),
        ],
        out_specs=pl.BlockSpec((S, dv), lambda b, h: (b, h)),
        out_shape=jax.ShapeDtypeStruct((T, MLSTM_WIDTH), BF16),
        scratch_shapes=[
            pltpu.VMEM((nc, 4 * CHUNK, dv), BF16),
            pltpu.VMEM((nc, 8, CHUNK), F32),
            pltpu.VMEM((2, nc, 8, CHUNK), F32),
            pltpu.VMEM((2, nc, dk, dv), F32),
            pltpu.VMEM((2, nc, 8, dk), F32),
            pltpu.VMEM((2, dk, dv), F32),
            pltpu.VMEM((2, 8, dk), F32),
        ],
        compiler_params=_cparams(("parallel", "arbitrary")),
        name="mlstm",
    )(proj, proj, proj, proj, proj, rows, norm_gain)


def _outproj_kernel(a_ref, m_ref, wa_ref, wm_ref, x_ref, y_ref):
    y = jnp.dot(a_ref[...], wa_ref[...], preferred_element_type=F32)
    y = y + jnp.dot(m_ref[...], wm_ref[...], preferred_element_type=F32)
    y_ref[...] = x_ref[...] + y


def _out_projection(attn, mlstm, w_out, x2, *, tm):
    T, D = x2.shape
    return pl.pallas_call(
        _outproj_kernel,
        grid=(T // tm,),
        in_specs=[
            pl.BlockSpec((tm, ATTN_WIDTH), lambda i: (i, 0)),
            pl.BlockSpec((tm, MLSTM_WIDTH), lambda i: (i, 0)),
            pl.BlockSpec((ATTN_WIDTH, D), lambda i: (0, 0)),
            pl.BlockSpec((MLSTM_WIDTH, D), lambda i: (1, 0)),
            pl.BlockSpec((tm, D), lambda i: (i, 0)),
        ],
        out_specs=pl.BlockSpec((tm, D), lambda i: (i, 0)),
        out_shape=jax.ShapeDtypeStruct((T, D), F32),
        compiler_params=_cparams(("parallel",)),
        name="outproj",
    )(attn, mlstm, w_out, w_out, x2)


def _rope_tables(S):
    row = (jnp.arange(S) // GRID_W).astype(F32)
    col = (jnp.arange(S) % GRID_W).astype(F32)
    nf = HEAD_DIM // 4
    inv = 1.0 / (ROPE_THETA ** (jnp.arange(nf, dtype=F32) / nf))
    ang_r = row[:, None] * inv
    ang_c = col[:, None] * inv
    ang = jnp.concatenate([ang_r, ang_r, ang_c, ang_c], axis=-1)
    cos, sin = jnp.cos(ang), jnp.sin(ang)
    quarter = (jnp.arange(HEAD_DIM) // nf) % 2
    sin_a = jnp.where(quarter == 0, -sin, 0.0)
    sin_b = jnp.where(quarter == 1, sin, 0.0)
    return cos, sin_a, sin_b


def _layer(x, norm_g, w_in, b_gates, q_norm_g, k_norm_g, mlstm_norm_g, w_out):
    B, S, D = x.shape
    T = B * S
    nc = S // CHUNK
    x2 = x.reshape(T, D)

    w_main = w_in[:, :N_MAIN].astype(BF16)
    w_gates = jnp.pad(w_in[:, N_MAIN:], ((0, 0), (0, LANES - N_GATE_COLS))).astype(BF16)
    tm = math.gcd(T, 256)
    bias = jnp.pad(b_gates.astype(F32), (0, LANES - N_GATE_COLS)).reshape(1, LANES)
    proj, gates = _in_projection(x2, norm_g.reshape(1, D), w_main, w_gates, bias, tm=tm, tn=512)

    cos, sin_a, sin_b = _rope_tables(S)
    attn = _attention(proj, cos, sin_a, sin_b, q_norm_g.reshape(1, HEAD_DIM),
                      k_norm_g.reshape(1, HEAD_DIM), B=B, S=S, tq=math.gcd(S, 256), tiles_per_step=4 if S % 1024 == 0 else 1)

    gates_rows = gates[:, :N_GATE_COLS].reshape(B, nc, CHUNK, 4, MLSTM_HEADS)
    gates_rows = gates_rows.transpose(3, 0, 4, 1, 2).reshape(4, B * MLSTM_HEADS * nc, CHUNK)
    mlstm = _mlstm(proj, _gate_rows(gates_rows), mlstm_norm_g.reshape(1, MLSTM_WIDTH), B=B, S=S)

    y = _out_projection(attn, mlstm, w_out.astype(BF16), x2, tm=math.gcd(T, 512))
    return y.reshape(B, S, D)


def kernel(x_prompt, x_sample, norm_g, w_in, b_gates, q_norm_g, k_norm_g, mlstm_norm_g, w_out):
    depth = norm_g.shape[0]
    outs = []
    for x in (x_prompt, x_sample):
        for l in range(depth):
            x = _layer(x, norm_g[l], w_in[l], b_gates[l], q_norm_g[l], k_norm_g[l],
                       mlstm_norm_g[l], w_out[l])
        outs.append(x)
    return tuple(outs)
```

```python
import functools
import math

import jax
import jax.numpy as jnp
from jax import lax
from jax.experimental import pallas as pl
from jax.experimental.pallas import tpu as pltpu

F32 = jnp.float32
BF16 = jnp.bfloat16

ATTN_HEADS = 8
ATTN_KV_HEADS = 2
HEAD_DIM = 128
ATTN_WIDTH = ATTN_HEADS * HEAD_DIM
KV_WIDTH = ATTN_KV_HEADS * HEAD_DIM
GROUP = ATTN_HEADS // ATTN_KV_HEADS
MLSTM_HEADS = 4
MLSTM_QK_DIM = 128
MLSTM_V_DIM = 256
MLSTM_QK_WIDTH = MLSTM_HEADS * MLSTM_QK_DIM
MLSTM_WIDTH = MLSTM_HEADS * MLSTM_V_DIM
N_GATE_COLS = 4 * MLSTM_HEADS
GRID_W = 64
ROPE_THETA = 10000.0
CHUNK = 128
EPS = 1e-6

OFF_AQ = 0
OFF_AK = OFF_AQ + ATTN_WIDTH
OFF_AV = OFF_AK + KV_WIDTH
OFF_AZ = OFF_AV + KV_WIDTH
OFF_MQ = OFF_AZ + ATTN_WIDTH
OFF_MK = OFF_MQ + MLSTM_QK_WIDTH
OFF_MV = OFF_MK + MLSTM_QK_WIDTH
OFF_MO = OFF_MV + MLSTM_WIDTH
OFF_MZ = OFF_MO + MLSTM_WIDTH
OFF_GATES = OFF_MZ + MLSTM_WIDTH
N_MAIN = OFF_GATES

LANES = 128
VMEM_LIMIT = 56 * 1024 * 1024


def _cparams(sem):
    return pltpu.CompilerParams(dimension_semantics=sem, vmem_limit_bytes=VMEM_LIMIT)


_NT = (((1,), (1,)), ((), ()))


def _sigmoid(x):
    return 0.5 * jnp.tanh(0.5 * x) + 0.5


def _silu(x):
    hx = 0.5 * x
    return hx * jnp.tanh(hx) + hx


def _norm_rope(x, gain, cos, sin_a, sin_b):
    ms = jnp.mean(x * x, axis=-1, keepdims=True)
    xn = x * lax.rsqrt(ms + EPS) * gain
    up = pltpu.roll(xn, 3 * HEAD_DIM // 4, axis=1)
    down = pltpu.roll(xn, HEAD_DIM // 4, axis=1)
    return xn * cos + up * sin_a + down * sin_b


def _inproj_kernel(x_ref, g_ref, w_ref, wg_ref, bg_ref, cos_ref, sa_ref, sb_ref, qg_ref, kg_ref,
                   o_ref, og_ref, h_ref, *, tn, row_chunk, seq_len):
    s = pl.program_id(0)
    tm, _ = x_ref.shape
    n_tiles = w_ref.shape[1] // tn
    n_chunks = tm // row_chunk
    nxt = s % 2
    cur = 1 - nxt

    @pl.when(s == 0)
    def _():
        h_ref[1] = jnp.zeros(h_ref.shape[1:], BF16)

    def normalise(r):
        rows = slice(r * row_chunk, (r + 1) * row_chunk)
        x = x_ref[rows, :]
        ms = jnp.mean(x * x, axis=-1, keepdims=True)
        h_ref[nxt, rows, :] = (x * lax.rsqrt(ms + EPS) * g_ref[...]).astype(BF16)

    pos = pl.ds(pl.multiple_of((jnp.maximum(s - 1, 0) * tm) % seq_len, tm), tm)
    qscale = math.log2(math.e) / math.sqrt(HEAD_DIM)

    for j in range(n_tiles):
        if j < n_chunks:
            normalise(j)
        acc = jnp.dot(h_ref[cur], w_ref[:, j * tn:(j + 1) * tn], preferred_element_type=F32)
        for c0 in range(j * tn, (j + 1) * tn, HEAD_DIM):
            piece = acc[:, c0 - j * tn:c0 - j * tn + HEAD_DIM]
            if c0 < OFF_AK:
                piece = _norm_rope(piece, qg_ref[...], cos_ref[pos, :], sa_ref[pos, :], sb_ref[pos, :]) * qscale
            elif c0 < OFF_AV:
                piece = _norm_rope(piece, kg_ref[...], cos_ref[pos, :], sa_ref[pos, :], sb_ref[pos, :])
            o_ref[:, c0:c0 + HEAD_DIM] = piece.astype(BF16)
    for r in range(n_tiles, n_chunks):
        normalise(r)
    gates = jnp.dot(h_ref[cur], wg_ref[...], preferred_element_type=F32) + bg_ref[...]
    for r in range(tm // LANES):
        og_ref[:, r * LANES:(r + 1) * LANES] = jnp.transpose(gates[r * LANES:(r + 1) * LANES, :])[:N_GATE_COLS, :]


def _in_projection(x2, norm_g, w_main, w_gates, b_gates, cos, sin_a, sin_b, q_gain, k_gain, *, tm, tn, seq_len):
    T, D = x2.shape
    N = w_main.shape[1]
    n_blocks = T // tm
    resident = pl.Buffered(1)
    const = lambda s: (0, 0)
    return pl.pallas_call(
        functools.partial(_inproj_kernel, tn=tn, row_chunk=32, seq_len=seq_len),
        grid=(n_blocks + 1,),
        in_specs=[
            pl.BlockSpec((tm, D), lambda s: (jnp.minimum(s, n_blocks - 1), 0)),
            pl.BlockSpec((1, D), const),
            pl.BlockSpec((D, N), const, pipeline_mode=resident),
            pl.BlockSpec((D, LANES), const, pipeline_mode=resident),
            pl.BlockSpec((1, LANES), const),
            pl.BlockSpec((seq_len, HEAD_DIM), const, pipeline_mode=resident),
            pl.BlockSpec((seq_len, HEAD_DIM), const, pipeline_mode=resident),
            pl.BlockSpec((seq_len, HEAD_DIM), const, pipeline_mode=resident),
            pl.BlockSpec((1, HEAD_DIM), const),
            pl.BlockSpec((1, HEAD_DIM), const),
        ],
        out_specs=[
            pl.BlockSpec((tm, N), lambda s: (jnp.maximum(s - 1, 0), 0)),
            pl.BlockSpec((N_GATE_COLS, tm), lambda s: (0, jnp.maximum(s - 1, 0))),
        ],
        out_shape=[
            jax.ShapeDtypeStruct((T, N), BF16),
            jax.ShapeDtypeStruct((N_GATE_COLS, T), F32),
        ],
        scratch_shapes=[pltpu.VMEM((2, tm, D), BF16)],
        compiler_params=_cparams(("arbitrary",)),
        name="inproj",
    )(x2, norm_g, w_main, w_gates, b_gates, cos, sin_a, sin_b, q_gain, k_gain)


def _attn_kernel(q_ref, k_ref, v_ref, z_ref, o_ref, vx_ref, *, tq):
    qi = pl.program_id(2)

    @pl.when(qi == 0)
    def _():
        vx_ref[:, :HEAD_DIM] = v_ref[...]
        vx_ref[:, HEAD_DIM:] = jnp.ones(v_ref.shape, BF16)

    block_rows = q_ref.shape[0]
    units = [(t, g) for t in range(block_rows // tq) for g in range(GROUP)]

    def scores(unit):
        t, g = unit
        q = q_ref[t * tq:(t + 1) * tq, g * HEAD_DIM:(g + 1) * HEAD_DIM]
        return lax.dot_general(q, k_ref[...], _NT, preferred_element_type=F32)

    s_next = scores(units[0])
    for i, (t, g) in enumerate(units):
        rows = slice(t * tq, (t + 1) * tq)
        cols = slice(g * HEAD_DIM, (g + 1) * HEAD_DIM)
        s = s_next
        if i + 1 < len(units):
            s_next = scores(units[i + 1])
        m = jnp.max(s, axis=-1, keepdims=True)
        p = jnp.exp2(s - m).astype(BF16)
        ox = jnp.dot(p, vx_ref[...], preferred_element_type=F32)
        o = ox[:, :HEAD_DIM] / ox[:, HEAD_DIM:]
        z = z_ref[rows, cols].astype(F32)
        o_ref[rows, cols] = (o * _silu(z)).astype(BF16)


def _attention(proj, *, B, S, tq, tiles_per_step):
    T = B * S
    bq = tq * tiles_per_step
    nq = S // bq
    gw = GROUP * HEAD_DIM
    return pl.pallas_call(
        functools.partial(_attn_kernel, tq=tq),
        grid=(B, ATTN_KV_HEADS, nq),
        in_specs=[
            pl.BlockSpec((bq, gw), lambda b, kv, qi: (b * nq + qi, OFF_AQ // gw + kv)),
            pl.BlockSpec((S, HEAD_DIM), lambda b, kv, qi: (b, OFF_AK // HEAD_DIM + kv)),
            pl.BlockSpec((S, HEAD_DIM), lambda b, kv, qi: (b, OFF_AV // HEAD_DIM + kv)),
            pl.BlockSpec((bq, gw), lambda b, kv, qi: (b * nq + qi, OFF_AZ // gw + kv)),
        ],
        out_specs=pl.BlockSpec((bq, gw), lambda b, kv, qi: (b * nq + qi, kv)),
        out_shape=jax.ShapeDtypeStruct((T, ATTN_WIDTH), BF16),
        scratch_shapes=[pltpu.VMEM((S, 2 * HEAD_DIM), BF16)],
        compiler_params=_cparams(("parallel", "parallel", "arbitrary")),
        name="attention",
    )(proj, proj, proj, proj)


def _split3(x):
    hi = x.astype(BF16)
    r1 = x - hi.astype(F32)
    mid = r1.astype(BF16)
    lo = (r1 - mid.astype(F32)).astype(BF16)
    return hi, mid, lo


def _log_sigmoid(x):
    return jnp.minimum(x, 0.0) - jnp.log1p(jnp.exp(-jnp.abs(x)))


def _cummax_lanes(x, reverse):
    n = x.shape[-1]
    lane = lax.broadcasted_iota(jnp.int32, x.shape, 1)
    k = 1
    while k < n:
        if reverse:
            shifted = jnp.where(lane < n - k, pltpu.roll(x, n - k, axis=1), -jnp.inf)
        else:
            shifted = jnp.where(lane >= k, pltpu.roll(x, k, axis=1), -jnp.inf)
        x = jnp.maximum(x, shifted)
        k *= 2
    return x


def _to_columns(row, n):
    return jnp.transpose(jnp.broadcast_to(row, (n, n)))


FWD, BWD = 0, 1
ROW_B, ROW_D, ROW_CM, ROW_WE, ROW_G, ROW_ML = range(6)
N_ROW_KINDS = 6


def _gate_rows_kernel(g_ref, out_ref):
    L = CHUNK
    n = g_ref.shape[1]
    t_idx = lax.broadcasted_iota(jnp.int32, (L, L), 0)
    s_idx = lax.broadcasted_iota(jnp.int32, (L, L), 1)
    for dr in (FWD, BWD):
        i_rows = g_ref[2 * dr]
        lf = _log_sigmoid(g_ref[2 * dr + 1])
        tri = jnp.where(t_idx >= s_idx if dr == BWD else t_idx <= s_idx, 1.0, 0.0).astype(BF16)
        b = sum(jnp.dot(p, tri, preferred_element_type=F32) for p in _split3(lf))
        d = i_rows - b
        g = b[:, 0:1] if dr == BWD else b[:, L - 1:L]
        e = g + d
        mloc = jnp.max(e, axis=-1, keepdims=True)
        out_ref[dr, ROW_B] = b
        out_ref[dr, ROW_D] = d
        out_ref[dr, ROW_CM] = _cummax_lanes(d, reverse=(dr == BWD))
        out_ref[dr, ROW_WE] = jnp.exp(e - mloc)
        out_ref[dr, ROW_G] = jnp.broadcast_to(g, (n, L))
        out_ref[dr, ROW_ML] = jnp.broadcast_to(mloc, (n, L))


def _gate_rows(gates_rows):
    n_rows = gates_rows.shape[1]
    rb = math.gcd(n_rows, 512)
    return pl.pallas_call(
        _gate_rows_kernel,
        grid=(n_rows // rb,),
        in_specs=[pl.BlockSpec((4, rb, CHUNK), lambda i: (0, i, 0))],
        out_specs=pl.BlockSpec((2, N_ROW_KINDS, rb, CHUNK), lambda i: (0, 0, i, 0)),
        out_shape=jax.ShapeDtypeStruct((2, N_ROW_KINDS, n_rows, CHUNK), F32),
        compiler_params=_cparams(("parallel",)),
        name="gate_rows",
    )(gates_rows)


def _mlstm_kernel(q_ref, k_ref, v_ref, o_ref, z_ref, row_ref, ng_ref, out_ref,
                  mix_ref, coef_ref, den_ref, u_ref, nu_ref, c_ref, n_ref, *, nc):
    L = CHUNK
    scale = MLSTM_QK_DIM ** -0.5
    B_, D_, CM_, WE_, G_, ML_ = ROW_B, ROW_D, ROW_CM, ROW_WE, ROW_G, ROW_ML

    t_idx = lax.broadcasted_iota(jnp.int32, (L, L), 0)
    s_idx = lax.broadcasted_iota(jnp.int32, (L, L), 1)
    lower = s_idx <= t_idx
    upper = s_idx >= t_idx
    eye = jnp.where(s_idx == t_idx, 1.0, 0.0).astype(BF16)

    ones8 = jnp.ones((8, L), BF16)

    def intra_group(gi, carry):
        cs = [gi * group + j for j in range(group)]
        rows = [pl.ds(pl.multiple_of(c * L, L), L) for c in cs]
        kts = [jnp.transpose(k_ref[r, :].astype(F32)).astype(BF16) for r in rows]
        scores = [jnp.dot(q_ref[r, :], kt, preferred_element_type=F32) for r, kt in zip(rows, kts)]
        for c, r, kt in zip(cs, rows, kts):
            ktf = kt.astype(F32)
            for dr in (FWD, BWD):
                we = row_ref[dr, WE_, pl.ds(c, 1), :]
                u_ref[dr, c] = jnp.dot((ktf * we).astype(BF16), v_ref[r, :], preferred_element_type=F32)
                we8 = jnp.broadcast_to(we, (8, L)).astype(BF16)
                nu_ref[dr, c] = jnp.dot(we8, k_ref[r, :], preferred_element_type=F32)
        qks = []
        for c, s in zip(cs, scores):
            for dr in (FWD, BWD):
                drow = row_ref[dr, D_, pl.ds(c, 1), :]
                cm_col = _to_columns(row_ref[dr, CM_, pl.ds(c, 1), :], L)
                w = jnp.exp(jnp.where(upper if dr == BWD else lower, drow - cm_col, -jnp.inf))
                qks.append((s * w).astype(BF16))
        for j, (c, r) in enumerate(zip(cs, rows)):
            for dr in (FWD, BWD):
                qk = qks[2 * j + dr]
                mix_ref[c, 2 * dr * L:(2 * dr + 1) * L, :] = jnp.dot(
                    qk, v_ref[r, :], preferred_element_type=F32).astype(BF16)
                den_ref[dr, c] = lax.dot_general(ones8, qk, _NT, preferred_element_type=F32)
        return carry
    group = math.gcd(nc, 8)
    lax.fori_loop(0, nc // group, intra_group, 0)

    c_ref[...] = jnp.zeros_like(c_ref)
    n_ref[...] = jnp.zeros_like(n_ref)

    def inter_group(gi, carry):
        m = list(carry)
        chains = []
        for j in range(pair):
            for dr in (FWD, BWD):
                c = gi * pair + j if dr == FWD else nc - 1 - (gi * pair + j)
                brow = row_ref[dr, B_, pl.ds(c, 1), :]
                inter_ = brow + m[dr]
                amax = brow + row_ref[dr, CM_, pl.ds(c, 1), :]
                m_t = jnp.maximum(inter_, amax)
                al = jnp.exp(amax - m_t) * scale
                be = jnp.exp(inter_ - m_t) * scale
                floor = jnp.exp(-m_t)
                g = row_ref[dr, G_, pl.ds(c, 1), :]
                ml = row_ref[dr, ML_, pl.ds(c, 1), :]
                m_new = jnp.maximum(g + m[dr], ml)
                sc = jnp.exp(g + m[dr] - m_new)
                su = jnp.exp(ml - m_new)
                c_old = c_ref[dr]
                n_old = n_ref[dr]
                c_ref[dr] = (jnp.concatenate([sc, sc], axis=1) * c_old
                             + jnp.concatenate([su, su], axis=1) * u_ref[dr, c])
                n_ref[dr] = sc * n_old + su * nu_ref[dr, c]
                m[dr] = m_new
                chains.append((dr, c, al, be, floor, c_old.astype(BF16), n_old.astype(BF16)))
        prods = []
        for dr, c, al, be, floor, c_bf, n_bf in chains:
            qc = q_ref[pl.ds(pl.multiple_of(c * L, L), L), :]
            qn = lax.dot_general(n_bf, qc, _NT, preferred_element_type=F32)[0:1]
            prods.append((qn, jnp.dot(qc, c_bf, preferred_element_type=F32)))
        for (dr, c, al, be, floor, _, _), (qn, q_c) in zip(chains, prods):
            den = al * den_ref[dr, c][0:1] + be * qn
            r = 1.0 / jnp.maximum(jnp.abs(den), floor)
            coef_ref[c, 2 * dr:2 * dr + 1, :] = al * r
            coef_ref[c, 2 * dr + 1:2 * dr + 2, :] = be * r
            mix_ref[c, (2 * dr + 1) * L:(2 * dr + 2) * L, :] = q_c.astype(BF16)
        return tuple(m)

    pair = math.gcd(nc, 8)
    zero = jnp.zeros((1, L), F32)
    lax.fori_loop(0, nc // pair, inter_group, (zero, zero))

    def fin_group(gi, carry):
        cs = [gi * fin + j for j in range(fin)]
        hs = []
        for c in cs:
            diags = [eye * jnp.broadcast_to(coef_ref[c, x:x + 1, :].astype(BF16), (L, L)) for x in range(4)]
            hs.append(jnp.dot(jnp.concatenate(diags, axis=1), mix_ref[c], preferred_element_type=F32))
        for c, h in zip(cs, hs):
            rows = pl.ds(pl.multiple_of(c * L, L), L)
            hm = h * _sigmoid(o_ref[rows, :].astype(F32))
            ms = jnp.mean(hm * hm, axis=-1, keepdims=True)
            hn = hm * lax.rsqrt(ms + EPS) * ng_ref[...]
            z = z_ref[rows, :].astype(F32)
            out_ref[rows, :] = (hn * _silu(z)).astype(BF16)
        return carry
    fin = math.gcd(nc, 4)
    lax.fori_loop(0, nc // fin, fin_group, 0)


def _mlstm(proj, rows, norm_gain, *, B, S):
    T = B * S
    nc = S // CHUNK
    dk, dv = MLSTM_QK_DIM, MLSTM_V_DIM
    return pl.pallas_call(
        functools.partial(_mlstm_kernel, nc=nc),
        grid=(B, MLSTM_HEADS),
        in_specs=[
            pl.BlockSpec((S, dk), lambda b, h: (b, OFF_MQ // dk + h)),
            pl.BlockSpec((S, dk), lambda b, h: (b, OFF_MK // dk + h)),
            pl.BlockSpec((S, dv), lambda b, h: (b, OFF_MV // dv + h)),
            pl.BlockSpec((S, dv), lambda b, h: (b, OFF_MO // dv + h)),
            pl.BlockSpec((S, dv), lambda b, h: (b, OFF_MZ // dv + h)),
            pl.BlockSpec((2, N_ROW_KINDS, nc, CHUNK), lambda b, h: (0, 0, h * B + b, 0)),
            pl.BlockSpec((1, dv), lambda b, h: (0, h)),
        ],
        out_specs=pl.BlockSpec((S, dv), lambda b, h: (b, h)),
        out_shape=jax.ShapeDtypeStruct((T, MLSTM_WIDTH), BF16),
        scratch_shapes=[
            pltpu.VMEM((nc, 4 * CHUNK, dv), BF16),
            pltpu.VMEM((nc, 8, CHUNK), F32),
            pltpu.VMEM((2, nc, 8, CHUNK), F32),
            pltpu.VMEM((2, nc, dk, dv), F32),
            pltpu.VMEM((2, nc, 8, dk), F32),
            pltpu.VMEM((2, dk, dv), F32),
            pltpu.VMEM((2, 8, dk), F32),
        ],
        compiler_params=_cparams(("parallel", "arbitrary")),
        name="mlstm",
    )(proj, proj, proj, proj, proj, rows, norm_gain)


def _outproj_kernel(a_ref, m_ref, wa_ref, wm_ref, x_ref, y_ref):
    y = jnp.dot(a_ref[...], wa_ref[...], preferred_element_type=F32)
    y = y + jnp.dot(m_ref[...], wm_ref[...], preferred_element_type=F32)
    y_ref[...] = x_ref[...] + y


def _out_projection(attn, mlstm, w_out, x2, *, tm):
    T, D = x2.shape
    return pl.pallas_call(
        _outproj_kernel,
        grid=(T // tm,),
        in_specs=[
            pl.BlockSpec((tm, ATTN_WIDTH), lambda i: (i, 0)),
            pl.BlockSpec((tm, MLSTM_WIDTH), lambda i: (i, 0)),
            pl.BlockSpec((ATTN_WIDTH, D), lambda i: (0, 0)),
            pl.BlockSpec((MLSTM_WIDTH, D), lambda i: (1, 0)),
            pl.BlockSpec((tm, D), lambda i: (i, 0)),
        ],
        out_specs=pl.BlockSpec((tm, D), lambda i: (i, 0)),
        out_shape=jax.ShapeDtypeStruct((T, D), F32),
        compiler_params=_cparams(("parallel",)),
        name="outproj",
    )(attn, mlstm, w_out, w_out, x2)


def _rope_tables(S):
    row = (jnp.arange(S) // GRID_W).astype(F32)
    col = (jnp.arange(S) % GRID_W).astype(F32)
    nf = HEAD_DIM // 4
    inv = 1.0 / (ROPE_THETA ** (jnp.arange(nf, dtype=F32) / nf))
    ang_r = row[:, None] * inv
    ang_c = col[:, None] * inv
    ang = jnp.concatenate([ang_r, ang_r, ang_c, ang_c], axis=-1)
    cos, sin = jnp.cos(ang), jnp.sin(ang)
    quarter = (jnp.arange(HEAD_DIM) // nf) % 2
    sin_a = jnp.where(quarter == 0, -sin, 0.0)
    sin_b = jnp.where(quarter == 1, sin, 0.0)
    return cos, sin_a, sin_b


def _layer(x, norm_g, w_in, b_gates, q_norm_g, k_norm_g, mlstm_norm_g, w_out):
    B, S, D = x.shape
    T = B * S
    nc = S // CHUNK
    x2 = x.reshape(T, D)

    w_main = w_in[:, :N_MAIN].astype(BF16)
    w_gates = jnp.pad(w_in[:, N_MAIN:], ((0, 0), (0, LANES - N_GATE_COLS))).astype(BF16)
    tm = math.gcd(S, 256)
    bias = jnp.pad(b_gates.astype(F32), (0, LANES - N_GATE_COLS)).reshape(1, LANES)
    cos, sin_a, sin_b = _rope_tables(S)
    proj, gates_t = _in_projection(x2, norm_g.reshape(1, D), w_main, w_gates, bias, cos, sin_a, sin_b,
                                   q_norm_g.reshape(1, HEAD_DIM), k_norm_g.reshape(1, HEAD_DIM),
                                   tm=tm, tn=512, seq_len=S)

    attn = _attention(proj, B=B, S=S, tq=math.gcd(S, 256), tiles_per_step=4 if S % 1024 == 0 else 1)

    gates_rows = gates_t.reshape(4, MLSTM_HEADS * B * nc, CHUNK)
    mlstm = _mlstm(proj, _gate_rows(gates_rows), mlstm_norm_g.reshape(1, MLSTM_WIDTH), B=B, S=S)

    y = _out_projection(attn, mlstm, w_out.astype(BF16), x2, tm=math.gcd(T, 512))
    return y.reshape(B, S, D)


def kernel(x_prompt, x_sample, norm_g, w_in, b_gates, q_norm_g, k_norm_g, mlstm_norm_g, w_out):
    depth = norm_g.shape[0]
    outs = []
    for x in (x_prompt, x_sample):
        for l in range(depth):
            x = _layer(x, norm_g[l], w_in[l], b_gates[l], q_norm_g[l], k_norm_g[l],
                       mlstm_norm_g[l], w_out[l])
        outs.append(x)
    return tuple(outs)
```

```python
import functools
import math

import jax
import jax.numpy as jnp
from jax import lax
from jax.experimental import pallas as pl
from jax.experimental.pallas import tpu as pltpu

F32 = jnp.float32
BF16 = jnp.bfloat16

ATTN_HEADS = 8
ATTN_KV_HEADS = 2
HEAD_DIM = 128
ATTN_WIDTH = ATTN_HEADS * HEAD_DIM
KV_WIDTH = ATTN_KV_HEADS * HEAD_DIM
GROUP = ATTN_HEADS // ATTN_KV_HEADS
MLSTM_HEADS = 4
MLSTM_QK_DIM = 128
MLSTM_V_DIM = 256
MLSTM_QK_WIDTH = MLSTM_HEADS * MLSTM_QK_DIM
MLSTM_WIDTH = MLSTM_HEADS * MLSTM_V_DIM
N_GATE_COLS = 4 * MLSTM_HEADS
GRID_W = 64
ROPE_THETA = 10000.0
CHUNK = 128
EPS = 1e-6

OFF_AQ = 0
OFF_AK = OFF_AQ + ATTN_WIDTH
OFF_AV = OFF_AK + KV_WIDTH
OFF_AZ = OFF_AV + KV_WIDTH
OFF_MQ = OFF_AZ + ATTN_WIDTH
OFF_MK = OFF_MQ + MLSTM_QK_WIDTH
OFF_MV = OFF_MK + MLSTM_QK_WIDTH
OFF_MO = OFF_MV + MLSTM_WIDTH
OFF_MZ = OFF_MO + MLSTM_WIDTH
OFF_GATES = OFF_MZ + MLSTM_WIDTH
N_MAIN = OFF_GATES

LANES = 128
VMEM_LIMIT = 56 * 1024 * 1024


def _cparams(sem):
    return pltpu.CompilerParams(dimension_semantics=sem, vmem_limit_bytes=VMEM_LIMIT)


_NT = (((1,), (1,)), ((), ()))


def _sigmoid(x):
    return 0.5 * jnp.tanh(0.5 * x) + 0.5


def _silu(x):
    hx = 0.5 * x
    return hx * jnp.tanh(hx) + hx


def _norm_rope(x, gain, cos, sin_a, sin_b):
    ms = jnp.mean(x * x, axis=-1, keepdims=True)
    xn = x * lax.rsqrt(ms + EPS) * gain
    up = pltpu.roll(xn, 3 * HEAD_DIM // 4, axis=1)
    down = pltpu.roll(xn, HEAD_DIM // 4, axis=1)
    return xn * cos + up * sin_a + down * sin_b


def _inproj_kernel(x_ref, g_ref, w_ref, wg_ref, bg_ref, cos_ref, sa_ref, sb_ref, qg_ref, kg_ref,
                   o_ref, og_ref, h_ref, *, tn, row_chunk, seq_len):
    s = pl.program_id(0)
    tm, _ = x_ref.shape
    n_tiles = w_ref.shape[1] // tn
    n_chunks = tm // row_chunk
    nxt = s % 2
    cur = 1 - nxt

    @pl.when(s == 0)
    def _():
        h_ref[1] = jnp.zeros(h_ref.shape[1:], BF16)

    def normalise(r):
        rows = slice(r * row_chunk, (r + 1) * row_chunk)
        x = x_ref[rows, :]
        ms = jnp.mean(x * x, axis=-1, keepdims=True)
        h_ref[nxt, rows, :] = (x * lax.rsqrt(ms + EPS) * g_ref[...]).astype(BF16)

    pos = pl.ds(pl.multiple_of((jnp.maximum(s - 1, 0) * tm) % seq_len, tm), tm)
    qscale = math.log2(math.e) / math.sqrt(HEAD_DIM)

    for j in range(n_tiles):
        if j < n_chunks:
            normalise(j)
        acc = jnp.dot(h_ref[cur], w_ref[:, j * tn:(j + 1) * tn], preferred_element_type=F32)
        for c0 in range(j * tn, (j + 1) * tn, HEAD_DIM):
            piece = acc[:, c0 - j * tn:c0 - j * tn + HEAD_DIM]
            if c0 < OFF_AK:
                piece = _norm_rope(piece, qg_ref[...], cos_ref[pos, :], sa_ref[pos, :], sb_ref[pos, :]) * qscale
            elif c0 < OFF_AV:
                piece = _norm_rope(piece, kg_ref[...], cos_ref[pos, :], sa_ref[pos, :], sb_ref[pos, :])
            o_ref[:, c0:c0 + HEAD_DIM] = piece.astype(BF16)
    for r in range(n_tiles, n_chunks):
        normalise(r)
    og_ref[...] = lax.dot_general(wg_ref[...], h_ref[cur], _NT, preferred_element_type=F32) + bg_ref[...]


def _in_projection(x2, norm_g, w_main, w_gates, b_gates, cos, sin_a, sin_b, q_gain, k_gain, *, tm, tn, seq_len):
    T, D = x2.shape
    N = N_MAIN
    n_blocks = T // tm
    resident = pl.Buffered(1)
    const = lambda s: (0, 0)
    return pl.pallas_call(
        functools.partial(_inproj_kernel, tn=tn, row_chunk=32, seq_len=seq_len),
        grid=(n_blocks + 1,),
        in_specs=[
            pl.BlockSpec((tm, D), lambda s: (jnp.minimum(s, n_blocks - 1), 0)),
            pl.BlockSpec((1, D), const),
            pl.BlockSpec((D, N_MAIN), const, pipeline_mode=resident),
            pl.BlockSpec((N_GATE_COLS, D), const),
            pl.BlockSpec((N_GATE_COLS, 1), const),
            pl.BlockSpec((seq_len, HEAD_DIM), const, pipeline_mode=resident),
            pl.BlockSpec((seq_len, HEAD_DIM), const, pipeline_mode=resident),
            pl.BlockSpec((seq_len, HEAD_DIM), const, pipeline_mode=resident),
            pl.BlockSpec((1, HEAD_DIM), const),
            pl.BlockSpec((1, HEAD_DIM), const),
        ],
        out_specs=[
            pl.BlockSpec((tm, N), lambda s: (jnp.maximum(s - 1, 0), 0)),
            pl.BlockSpec((N_GATE_COLS, tm), lambda s: (0, jnp.maximum(s - 1, 0))),
        ],
        out_shape=[
            jax.ShapeDtypeStruct((T, N), BF16),
            jax.ShapeDtypeStruct((N_GATE_COLS, T), F32),
        ],
        scratch_shapes=[pltpu.VMEM((2, tm, D), BF16)],
        compiler_params=_cparams(("arbitrary",)),
        name="inproj",
    )(x2, norm_g, w_main, w_gates, b_gates, cos, sin_a, sin_b, q_gain, k_gain)


def _attn_kernel(q_ref, k_ref, v_ref, z_ref, o_ref, vx_ref, *, tq):
    qi = pl.program_id(2)

    @pl.when(qi == 0)
    def _():
        vx_ref[:, :HEAD_DIM] = v_ref[...]
        vx_ref[:, HEAD_DIM:] = jnp.ones(v_ref.shape, BF16)

    block_rows = q_ref.shape[0]
    units = [(t, g) for t in range(block_rows // tq) for g in range(GROUP)]

    def scores(unit):
        t, g = unit
        q = q_ref[t * tq:(t + 1) * tq, g * HEAD_DIM:(g + 1) * HEAD_DIM]
        return lax.dot_general(q, k_ref[...], _NT, preferred_element_type=F32)

    s_next = scores(units[0])
    for i, (t, g) in enumerate(units):
        rows = slice(t * tq, (t + 1) * tq)
        cols = slice(g * HEAD_DIM, (g + 1) * HEAD_DIM)
        s = s_next
        if i + 1 < len(units):
            s_next = scores(units[i + 1])
        m = jnp.max(s, axis=-1, keepdims=True)
        p = jnp.exp2(s - m).astype(BF16)
        ox = jnp.dot(p, vx_ref[...], preferred_element_type=F32)
        o = ox[:, :HEAD_DIM] / ox[:, HEAD_DIM:]
        z = z_ref[rows, cols].astype(F32)
        o_ref[rows, cols] = (o * _silu(z)).astype(BF16)


def _attention(proj, *, B, S, tq, tiles_per_step):
    T = B * S
    bq = tq * tiles_per_step
    nq = S // bq
    gw = GROUP * HEAD_DIM
    return pl.pallas_call(
        functools.partial(_attn_kernel, tq=tq),
        grid=(B, ATTN_KV_HEADS, nq),
        in_specs=[
            pl.BlockSpec((bq, gw), lambda b, kv, qi: (b * nq + qi, OFF_AQ // gw + kv)),
            pl.BlockSpec((S, HEAD_DIM), lambda b, kv, qi: (b, OFF_AK // HEAD_DIM + kv)),
            pl.BlockSpec((S, HEAD_DIM), lambda b, kv, qi: (b, OFF_AV // HEAD_DIM + kv)),
            pl.BlockSpec((bq, gw), lambda b, kv, qi: (b * nq + qi, OFF_AZ // gw + kv)),
        ],
        out_specs=pl.BlockSpec((bq, gw), lambda b, kv, qi: (b * nq + qi, kv)),
        out_shape=jax.ShapeDtypeStruct((T, ATTN_WIDTH), BF16),
        scratch_shapes=[pltpu.VMEM((S, 2 * HEAD_DIM), BF16)],
        compiler_params=_cparams(("parallel", "parallel", "arbitrary")),
        name="attention",
    )(proj, proj, proj, proj)


def _split3(x):
    hi = x.astype(BF16)
    r1 = x - hi.astype(F32)
    mid = r1.astype(BF16)
    lo = (r1 - mid.astype(F32)).astype(BF16)
    return hi, mid, lo


def _log_sigmoid(x):
    return jnp.minimum(x, 0.0) - jnp.log1p(jnp.exp(-jnp.abs(x)))


def _cummax_lanes(x, reverse):
    n = x.shape[-1]
    lane = lax.broadcasted_iota(jnp.int32, x.shape, 1)
    k = 1
    while k < n:
        if reverse:
            shifted = jnp.where(lane < n - k, pltpu.roll(x, n - k, axis=1), -jnp.inf)
        else:
            shifted = jnp.where(lane >= k, pltpu.roll(x, k, axis=1), -jnp.inf)
        x = jnp.maximum(x, shifted)
        k *= 2
    return x


def _to_columns(row, n):
    return jnp.transpose(jnp.broadcast_to(row, (n, n)))


FWD, BWD = 0, 1
ROW_B, ROW_D, ROW_CM, ROW_WE, ROW_G, ROW_ML = range(6)
N_ROW_KINDS = 6


def _gate_rows_kernel(g_ref, out_ref):
    L = CHUNK
    n = g_ref.shape[1]
    t_idx = lax.broadcasted_iota(jnp.int32, (L, L), 0)
    s_idx = lax.broadcasted_iota(jnp.int32, (L, L), 1)
    for dr in (FWD, BWD):
        i_rows = g_ref[2 * dr]
        lf = _log_sigmoid(g_ref[2 * dr + 1])
        tri = jnp.where(t_idx >= s_idx if dr == BWD else t_idx <= s_idx, 1.0, 0.0).astype(BF16)
        b = sum(jnp.dot(p, tri, preferred_element_type=F32) for p in _split3(lf))
        d = i_rows - b
        g = b[:, 0:1] if dr == BWD else b[:, L - 1:L]
        e = g + d
        mloc = jnp.max(e, axis=-1, keepdims=True)
        out_ref[dr, ROW_B] = b
        out_ref[dr, ROW_D] = d
        out_ref[dr, ROW_CM] = _cummax_lanes(d, reverse=(dr == BWD))
        out_ref[dr, ROW_WE] = jnp.exp(e - mloc)
        out_ref[dr, ROW_G] = jnp.broadcast_to(g, (n, L))
        out_ref[dr, ROW_ML] = jnp.broadcast_to(mloc, (n, L))


def _gate_rows(gates_rows):
    n_rows = gates_rows.shape[1]
    rb = math.gcd(n_rows, 512)
    return pl.pallas_call(
        _gate_rows_kernel,
        grid=(n_rows // rb,),
        in_specs=[pl.BlockSpec((4, rb, CHUNK), lambda i: (0, i, 0))],
        out_specs=pl.BlockSpec((2, N_ROW_KINDS, rb, CHUNK), lambda i: (0, 0, i, 0)),
        out_shape=jax.ShapeDtypeStruct((2, N_ROW_KINDS, n_rows, CHUNK), F32),
        compiler_params=_cparams(("parallel",)),
        name="gate_rows",
    )(gates_rows)


def _mlstm_kernel(q_ref, k_ref, v_ref, o_ref, z_ref, row_ref, ng_ref, out_ref,
                  mix_ref, coef_ref, den_ref, u_ref, nu_ref, c_ref, n_ref, *, nc):
    L = CHUNK
    scale = MLSTM_QK_DIM ** -0.5
    B_, D_, CM_, WE_, G_, ML_ = ROW_B, ROW_D, ROW_CM, ROW_WE, ROW_G, ROW_ML

    t_idx = lax.broadcasted_iota(jnp.int32, (L, L), 0)
    s_idx = lax.broadcasted_iota(jnp.int32, (L, L), 1)
    lower = s_idx <= t_idx
    upper = s_idx >= t_idx
    eye = jnp.where(s_idx == t_idx, 1.0, 0.0).astype(BF16)

    ones8 = jnp.ones((8, L), BF16)

    def chunk_rows(c):
        return slice(c * L, (c + 1) * L)

    def intra_group(gi, carry):
        cs = [gi * group + j for j in range(group)]
        rows = [chunk_rows(c) for c in cs]
        kts = [jnp.transpose(k_ref[r, :].astype(F32)).astype(BF16) for r in rows]
        scores = [jnp.dot(q_ref[r, :], kt, preferred_element_type=F32) for r, kt in zip(rows, kts)]
        for c, r, kt in zip(cs, rows, kts):
            ktf = kt.astype(F32)
            for dr in (FWD, BWD):
                we = row_ref[dr, WE_, pl.ds(c, 1), :]
                u_ref[dr, c] = jnp.dot((ktf * we).astype(BF16), v_ref[r, :], preferred_element_type=F32)
                we8 = jnp.broadcast_to(we, (8, L)).astype(BF16)
                nu_ref[dr, c] = jnp.dot(we8, k_ref[r, :], preferred_element_type=F32)
        qks = []
        for c, s in zip(cs, scores):
            for dr in (FWD, BWD):
                drow = row_ref[dr, D_, pl.ds(c, 1), :]
                cm_col = _to_columns(row_ref[dr, CM_, pl.ds(c, 1), :], L)
                w = jnp.exp(jnp.where(upper if dr == BWD else lower, drow - cm_col, -jnp.inf))
                qks.append((s * w).astype(BF16))
        for j, (c, r) in enumerate(zip(cs, rows)):
            for dr in (FWD, BWD):
                qk = qks[2 * j + dr]
                mix_ref[c, 2 * dr * L:(2 * dr + 1) * L, :] = jnp.dot(
                    qk, v_ref[r, :], preferred_element_type=F32).astype(BF16)
                den_ref[dr, c] = lax.dot_general(ones8, qk, _NT, preferred_element_type=F32)
        return carry
    group = math.gcd(nc, 2)
    for gi in range(nc // group):
        intra_group(gi, 0)

    c_ref[...] = jnp.zeros_like(c_ref)
    n_ref[...] = jnp.zeros_like(n_ref)

    def inter_group(gi, carry):
        m = list(carry)
        chains = []
        for j in range(pair):
            for dr in (FWD, BWD):
                c = gi * pair + j if dr == FWD else nc - 1 - (gi * pair + j)
                brow = row_ref[dr, B_, pl.ds(c, 1), :]
                inter_ = brow + m[dr]
                amax = brow + row_ref[dr, CM_, pl.ds(c, 1), :]
                m_t = jnp.maximum(inter_, amax)
                al = jnp.exp(amax - m_t) * scale
                be = jnp.exp(inter_ - m_t) * scale
                floor = jnp.exp(-m_t)
                g = row_ref[dr, G_, pl.ds(c, 1), :]
                ml = row_ref[dr, ML_, pl.ds(c, 1), :]
                m_new = jnp.maximum(g + m[dr], ml)
                sc = jnp.exp(g + m[dr] - m_new)
                su = jnp.exp(ml - m_new)
                c_old = c_ref[dr]
                n_old = n_ref[dr]
                c_ref[dr] = (jnp.concatenate([sc, sc], axis=1) * c_old
                             + jnp.concatenate([su, su], axis=1) * u_ref[dr, c])
                n_ref[dr] = sc * n_old + su * nu_ref[dr, c]
                m[dr] = m_new
                chains.append((dr, c, al, be, floor, c_old.astype(BF16), n_old.astype(BF16)))
        prods = []
        for dr, c, al, be, floor, c_bf, n_bf in chains:
            qc = q_ref[chunk_rows(c), :]
            qn = lax.dot_general(n_bf, qc, _NT, preferred_element_type=F32)[0:1]
            prods.append((qn, jnp.dot(qc, c_bf, preferred_element_type=F32)))
        for (dr, c, al, be, floor, _, _), (qn, q_c) in zip(chains, prods):
            den = al * den_ref[dr, c][0:1] + be * qn
            r = 1.0 / jnp.maximum(jnp.abs(den), floor)
            coef_ref[c, 2 * dr:2 * dr + 1, :] = al * r
            coef_ref[c, 2 * dr + 1:2 * dr + 2, :] = be * r
            mix_ref[c, (2 * dr + 1) * L:(2 * dr + 2) * L, :] = q_c.astype(BF16)
        return tuple(m)

    pair = math.gcd(nc, 4)
    zero = jnp.zeros((1, L), F32)
    m_carry = (zero, zero)
    for gi in range(nc // pair):
        m_carry = inter_group(gi, m_carry)

    def fin_group(gi, carry):
        cs = [gi * fin + j for j in range(fin)]
        hs = []
        for c in cs:
            diags = [eye * jnp.broadcast_to(coef_ref[c, x:x + 1, :].astype(BF16), (L, L)) for x in range(4)]
            hs.append(jnp.dot(jnp.concatenate(diags, axis=1), mix_ref[c], preferred_element_type=F32))
        for c, h in zip(cs, hs):
            rows = chunk_rows(c)
            hm = h * _sigmoid(o_ref[rows, :].astype(F32))
            ms = jnp.mean(hm * hm, axis=-1, keepdims=True)
            hn = hm * lax.rsqrt(ms + EPS) * ng_ref[...]
            z = z_ref[rows, :].astype(F32)
            out_ref[rows, :] = (hn * _silu(z)).astype(BF16)
        return carry
    fin = math.gcd(nc, 4)
    for gi in range(nc // fin):
        fin_group(gi, 0)


def _mlstm(proj, rows, norm_gain, *, B, S):
    T = B * S
    nc = S // CHUNK
    dk, dv = MLSTM_QK_DIM, MLSTM_V_DIM
    return pl.pallas_call(
        functools.partial(_mlstm_kernel, nc=nc),
        grid=(B, MLSTM_HEADS),
        in_specs=[
            pl.BlockSpec((S, dk), lambda b, h: (b, OFF_MQ // dk + h)),
            pl.BlockSpec((S, dk), lambda b, h: (b, OFF_MK // dk + h)),
            pl.BlockSpec((S, dv), lambda b, h: (b, OFF_MV // dv + h)),
            pl.BlockSpec((S, dv), lambda b, h: (b, OFF_MO // dv + h)),
            pl.BlockSpec((S, dv), lambda b, h: (b, OFF_MZ // dv + h)),
            pl.BlockSpec((2, N_ROW_KINDS, nc, CHUNK), lambda b, h: (0, 0, h * B + b, 0)),
            pl.BlockSpec((1, dv), lambda b, h: (0, h)),
        ],
        out_specs=pl.BlockSpec((S, dv), lambda b, h: (b, h)),
        out_shape=jax.ShapeDtypeStruct((T, MLSTM_WIDTH), BF16),
        scratch_shapes=[
            pltpu.VMEM((nc, 4 * CHUNK, dv), BF16),
            pltpu.VMEM((nc, 8, CHUNK), F32),
            pltpu.VMEM((2, nc, 8, CHUNK), F32),
            pltpu.VMEM((2, nc, dk, dv), F32),
            pltpu.VMEM((2, nc, 8, dk), F32),
            pltpu.VMEM((2, dk, dv), F32),
            pltpu.VMEM((2, 8, dk), F32),
        ],
        compiler_params=_cparams(("parallel", "arbitrary")),
        name="mlstm",
    )(proj, proj, proj, proj, proj, rows, norm_gain)


def _outproj_kernel(a_ref, m_ref, wa_ref, wm_ref, x_ref, y_ref):
    y = jnp.dot(a_ref[...], wa_ref[...], preferred_element_type=F32)
    y = y + jnp.dot(m_ref[...], wm_ref[...], preferred_element_type=F32)
    y_ref[...] = x_ref[...] + y


def _out_projection(attn, mlstm, w_out, x2, *, tm):
    T, D = x2.shape
    return pl.pallas_call(
        _outproj_kernel,
        grid=(T // tm,),
        in_specs=[
            pl.BlockSpec((tm, ATTN_WIDTH), lambda i: (i, 0)),
            pl.BlockSpec((tm, MLSTM_WIDTH), lambda i: (i, 0)),
            pl.BlockSpec((ATTN_WIDTH, D), lambda i: (0, 0)),
            pl.BlockSpec((MLSTM_WIDTH, D), lambda i: (1, 0)),
            pl.BlockSpec((tm, D), lambda i: (i, 0)),
        ],
        out_specs=pl.BlockSpec((tm, D), lambda i: (i, 0)),
        out_shape=jax.ShapeDtypeStruct((T, D), F32),
        compiler_params=_cparams(("parallel",)),
        name="outproj",
    )(attn, mlstm, w_out, w_out, x2)


def _rope_tables(S):
    row = (jnp.arange(S) // GRID_W).astype(F32)
    col = (jnp.arange(S) % GRID_W).astype(F32)
    nf = HEAD_DIM // 4
    inv = 1.0 / (ROPE_THETA ** (jnp.arange(nf, dtype=F32) / nf))
    ang_r = row[:, None] * inv
    ang_c = col[:, None] * inv
    ang = jnp.concatenate([ang_r, ang_r, ang_c, ang_c], axis=-1)
    cos, sin = jnp.cos(ang), jnp.sin(ang)
    quarter = (jnp.arange(HEAD_DIM) // nf) % 2
    sin_a = jnp.where(quarter == 0, -sin, 0.0)
    sin_b = jnp.where(quarter == 1, sin, 0.0)
    return cos, sin_a, sin_b


def _layer(x, norm_g, w_in, b_gates, q_norm_g, k_norm_g, mlstm_norm_g, w_out):
    B, S, D = x.shape
    T = B * S
    nc = S // CHUNK
    x2 = x.reshape(T, D)

    w_main = w_in.astype(BF16)
    w_gates = w_in[:, N_MAIN:].T.astype(BF16)
    tm = math.gcd(S, 256)
    bias = b_gates.astype(F32).reshape(N_GATE_COLS, 1)
    cos, sin_a, sin_b = _rope_tables(S)
    proj, gates_t = _in_projection(x2, norm_g.reshape(1, D), w_main, w_gates, bias, cos, sin_a, sin_b,
                                   q_norm_g.reshape(1, HEAD_DIM), k_norm_g.reshape(1, HEAD_DIM),
                                   tm=tm, tn=512, seq_len=S)

    attn = _attention(proj, B=B, S=S, tq=math.gcd(S, 256), tiles_per_step=4 if S % 1024 == 0 else 1)

    gates_rows = gates_t.reshape(4, MLSTM_HEADS * B * nc, CHUNK)
    mlstm = _mlstm(proj, _gate_rows(gates_rows), mlstm_norm_g.reshape(1, MLSTM_WIDTH), B=B, S=S)

    y = _out_projection(attn, mlstm, w_out.astype(BF16), x2, tm=math.gcd(T, 512))
    return y.reshape(B, S, D)


def kernel(x_prompt, x_sample, norm_g, w_in, b_gates, q_norm_g, k_norm_g, mlstm_norm_g, w_out):
    depth = norm_g.shape[0]
    outs = []
    for x in (x_prompt, x_sample):
        for l in range(depth):
            x = _layer(x, norm_g[l], w_in[l], b_gates[l], q_norm_g[l], k_norm_g[l],
                       mlstm_norm_g[l], w_out[l])
        outs.append(x)
    return tuple(outs)
```

```python
import functools
import math

import jax
import jax.numpy as jnp
from jax import lax
from jax.experimental import pallas as pl
from jax.experimental.pallas import tpu as pltpu

F32 = jnp.float32
BF16 = jnp.bfloat16

ATTN_HEADS = 8
ATTN_KV_HEADS = 2
HEAD_DIM = 128
ATTN_WIDTH = ATTN_HEADS * HEAD_DIM
KV_WIDTH = ATTN_KV_HEADS * HEAD_DIM
GROUP = ATTN_HEADS // ATTN_KV_HEADS
MLSTM_HEADS = 4
MLSTM_QK_DIM = 128
MLSTM_V_DIM = 256
MLSTM_QK_WIDTH = MLSTM_HEADS * MLSTM_QK_DIM
MLSTM_WIDTH = MLSTM_HEADS * MLSTM_V_DIM
N_GATE_COLS = 4 * MLSTM_HEADS
GRID_W = 64
ROPE_THETA = 10000.0
CHUNK = 128
EPS = 1e-6

OFF_AQ = 0
OFF_AK = OFF_AQ + ATTN_WIDTH
OFF_AV = OFF_AK + KV_WIDTH
OFF_AZ = OFF_AV + KV_WIDTH
OFF_MQ = OFF_AZ + ATTN_WIDTH
OFF_MK = OFF_MQ + MLSTM_QK_WIDTH
OFF_MV = OFF_MK + MLSTM_QK_WIDTH
OFF_MO = OFF_MV + MLSTM_WIDTH
OFF_MZ = OFF_MO + MLSTM_WIDTH
OFF_GATES = OFF_MZ + MLSTM_WIDTH
N_MAIN = OFF_GATES

LANES = 128
VMEM_LIMIT = 56 * 1024 * 1024


def _cparams(sem):
    return pltpu.CompilerParams(dimension_semantics=sem, vmem_limit_bytes=VMEM_LIMIT)


_NT = (((1,), (1,)), ((), ()))


def _sigmoid(x):
    return 0.5 * jnp.tanh(0.5 * x) + 0.5


def _silu(x):
    hx = 0.5 * x
    return hx * jnp.tanh(hx) + hx


def _norm_rope(x, gain, cos, sin_a, sin_b):
    ms = jnp.mean(x * x, axis=-1, keepdims=True)
    xn = x * lax.rsqrt(ms + EPS) * gain
    up = pltpu.roll(xn, 3 * HEAD_DIM // 4, axis=1)
    down = pltpu.roll(xn, HEAD_DIM // 4, axis=1)
    return xn * cos + up * sin_a + down * sin_b


def _inproj_kernel(x_ref, g_ref, w_ref, wg_ref, bg_ref, cos_ref, sa_ref, sb_ref, qg_ref, kg_ref,
                   o_ref, og_ref, h_ref, *, tn, row_chunk, seq_len):
    s = pl.program_id(0)
    tm, _ = x_ref.shape
    n_tiles = w_ref.shape[1] // tn
    n_chunks = tm // row_chunk
    nxt = s % 2
    cur = 1 - nxt

    @pl.when(s == 0)
    def _():
        h_ref[1] = jnp.zeros(h_ref.shape[1:], BF16)

    def normalise(r):
        rows = slice(r * row_chunk, (r + 1) * row_chunk)
        x = x_ref[rows, :]
        ms = jnp.mean(x * x, axis=-1, keepdims=True)
        h_ref[nxt, rows, :] = (x * lax.rsqrt(ms + EPS) * g_ref[...]).astype(BF16)

    pos = pl.ds(pl.multiple_of((jnp.maximum(s - 1, 0) * tm) % seq_len, tm), tm)
    qscale = math.log2(math.e) / math.sqrt(HEAD_DIM)

    for j in range(n_tiles):
        if j < n_chunks:
            normalise(j)
        acc = jnp.dot(h_ref[cur], w_ref[:, j * tn:(j + 1) * tn], preferred_element_type=F32)
        for c0 in range(j * tn, (j + 1) * tn, HEAD_DIM):
            piece = acc[:, c0 - j * tn:c0 - j * tn + HEAD_DIM]
            if c0 < OFF_AK:
                piece = _norm_rope(piece, qg_ref[...], cos_ref[pos, :], sa_ref[pos, :], sb_ref[pos, :]) * qscale
            elif c0 < OFF_AV:
                piece = _norm_rope(piece, kg_ref[...], cos_ref[pos, :], sa_ref[pos, :], sb_ref[pos, :])
            o_ref[:, c0:c0 + HEAD_DIM] = piece.astype(BF16)
    for r in range(n_tiles, n_chunks):
        normalise(r)
    og_ref[...] = lax.dot_general(wg_ref[...], h_ref[cur], _NT, preferred_element_type=F32) + bg_ref[...]


def _in_projection(x2, norm_g, w_main, w_gates, b_gates, cos, sin_a, sin_b, q_gain, k_gain, *, tm, tn, seq_len):
    T, D = x2.shape
    N = N_MAIN
    n_blocks = T // tm
    resident = pl.Buffered(1)
    const = lambda s: (0, 0)
    return pl.pallas_call(
        functools.partial(_inproj_kernel, tn=tn, row_chunk=32, seq_len=seq_len),
        grid=(n_blocks + 1,),
        in_specs=[
            pl.BlockSpec((tm, D), lambda s: (jnp.minimum(s, n_blocks - 1), 0)),
            pl.BlockSpec((1, D), const),
            pl.BlockSpec((D, N_MAIN), const, pipeline_mode=resident),
            pl.BlockSpec((N_GATE_COLS, D), const),
            pl.BlockSpec((N_GATE_COLS, 1), const),
            pl.BlockSpec((seq_len, HEAD_DIM), const, pipeline_mode=resident),
            pl.BlockSpec((seq_len, HEAD_DIM), const, pipeline_mode=resident),
            pl.BlockSpec((seq_len, HEAD_DIM), const, pipeline_mode=resident),
            pl.BlockSpec((1, HEAD_DIM), const),
            pl.BlockSpec((1, HEAD_DIM), const),
        ],
        out_specs=[
            pl.BlockSpec((tm, N), lambda s: (jnp.maximum(s - 1, 0), 0)),
            pl.BlockSpec((N_GATE_COLS, tm), lambda s: (0, jnp.maximum(s - 1, 0))),
        ],
        out_shape=[
            jax.ShapeDtypeStruct((T, N), BF16),
            jax.ShapeDtypeStruct((N_GATE_COLS, T), F32),
        ],
        scratch_shapes=[pltpu.VMEM((2, tm, D), BF16)],
        compiler_params=_cparams(("arbitrary",)),
        name="inproj",
    )(x2, norm_g, w_main, w_gates, b_gates, cos, sin_a, sin_b, q_gain, k_gain)


def _attn_kernel(q_ref, k_ref, v_ref, z_ref, o_ref, vx_ref, *, tq):
    qi = pl.program_id(2)

    @pl.when(qi == 0)
    def _():
        vx_ref[:, :HEAD_DIM] = v_ref[...]
        vx_ref[:, HEAD_DIM:] = jnp.ones(v_ref.shape, BF16)

    block_rows = q_ref.shape[0]
    units = [(t, g) for t in range(block_rows // tq) for g in range(GROUP)]

    def scores(unit):
        t, g = unit
        q = q_ref[t * tq:(t + 1) * tq, g * HEAD_DIM:(g + 1) * HEAD_DIM]
        return lax.dot_general(q, k_ref[...], _NT, preferred_element_type=F32)

    s_next = scores(units[0])
    for i, (t, g) in enumerate(units):
        rows = slice(t * tq, (t + 1) * tq)
        cols = slice(g * HEAD_DIM, (g + 1) * HEAD_DIM)
        s = s_next
        if i + 1 < len(units):
            s_next = scores(units[i + 1])
        m = jnp.max(s, axis=-1, keepdims=True)
        p = jnp.exp2(s - m).astype(BF16)
        ox = jnp.dot(p, vx_ref[...], preferred_element_type=F32)
        o = ox[:, :HEAD_DIM] / ox[:, HEAD_DIM:]
        z = z_ref[rows, cols].astype(F32)
        o_ref[rows, cols] = (o * _silu(z)).astype(BF16)


def _attention(proj, *, B, S, tq, tiles_per_step):
    T = B * S
    bq = tq * tiles_per_step
    nq = S // bq
    gw = GROUP * HEAD_DIM
    return pl.pallas_call(
        functools.partial(_attn_kernel, tq=tq),
        grid=(B, ATTN_KV_HEADS, nq),
        in_specs=[
            pl.BlockSpec((bq, gw), lambda b, kv, qi: (b * nq + qi, OFF_AQ // gw + kv)),
            pl.BlockSpec((S, HEAD_DIM), lambda b, kv, qi: (b, OFF_AK // HEAD_DIM + kv)),
            pl.BlockSpec((S, HEAD_DIM), lambda b, kv, qi: (b, OFF_AV // HEAD_DIM + kv)),
            pl.BlockSpec((bq, gw), lambda b, kv, qi: (b * nq + qi, OFF_AZ // gw + kv)),
        ],
        out_specs=pl.BlockSpec((bq, gw), lambda b, kv, qi: (b * nq + qi, kv)),
        out_shape=jax.ShapeDtypeStruct((T, ATTN_WIDTH), BF16),
        scratch_shapes=[pltpu.VMEM((S, 2 * HEAD_DIM), BF16)],
        compiler_params=_cparams(("parallel", "parallel", "arbitrary")),
        name="attention",
    )(proj, proj, proj, proj)


def _split3(x):
    hi = x.astype(BF16)
    r1 = x - hi.astype(F32)
    mid = r1.astype(BF16)
    lo = (r1 - mid.astype(F32)).astype(BF16)
    return hi, mid, lo


def _log_sigmoid(x):
    return jnp.minimum(x, 0.0) - jnp.log1p(jnp.exp(-jnp.abs(x)))


def _cummax_lanes(x, reverse):
    n = x.shape[-1]
    lane = lax.broadcasted_iota(jnp.int32, x.shape, 1)
    k = 1
    while k < n:
        if reverse:
            shifted = jnp.where(lane < n - k, pltpu.roll(x, n - k, axis=1), -jnp.inf)
        else:
            shifted = jnp.where(lane >= k, pltpu.roll(x, k, axis=1), -jnp.inf)
        x = jnp.maximum(x, shifted)
        k *= 2
    return x


def _to_columns(row, n):
    return jnp.transpose(jnp.broadcast_to(row, (n, n)))


FWD, BWD = 0, 1
ROW_B, ROW_D, ROW_CM, ROW_WE, ROW_G, ROW_ML = range(6)
N_ROW_KINDS = 6


def _gate_rows_kernel(g_ref, out_ref):
    L = CHUNK
    n = g_ref.shape[1]
    t_idx = lax.broadcasted_iota(jnp.int32, (L, L), 0)
    s_idx = lax.broadcasted_iota(jnp.int32, (L, L), 1)
    for dr in (FWD, BWD):
        i_rows = g_ref[2 * dr]
        lf = _log_sigmoid(g_ref[2 * dr + 1])
        tri = jnp.where(t_idx >= s_idx if dr == BWD else t_idx <= s_idx, 1.0, 0.0).astype(BF16)
        b = sum(jnp.dot(p, tri, preferred_element_type=F32) for p in _split3(lf))
        d = i_rows - b
        g = b[:, 0:1] if dr == BWD else b[:, L - 1:L]
        e = g + d
        mloc = jnp.max(e, axis=-1, keepdims=True)
        out_ref[dr, ROW_B] = b
        out_ref[dr, ROW_D] = d
        out_ref[dr, ROW_CM] = _cummax_lanes(d, reverse=(dr == BWD))
        out_ref[dr, ROW_WE] = jnp.exp(e - mloc)
        out_ref[dr, ROW_G] = jnp.broadcast_to(g, (n, L))
        out_ref[dr, ROW_ML] = jnp.broadcast_to(mloc, (n, L))


def _gate_rows(gates_rows):
    n_rows = gates_rows.shape[1]
    rb = math.gcd(n_rows, 512)
    return pl.pallas_call(
        _gate_rows_kernel,
        grid=(n_rows // rb,),
        in_specs=[pl.BlockSpec((4, rb, CHUNK), lambda i: (0, i, 0))],
        out_specs=pl.BlockSpec((2, N_ROW_KINDS, rb, CHUNK), lambda i: (0, 0, i, 0)),
        out_shape=jax.ShapeDtypeStruct((2, N_ROW_KINDS, n_rows, CHUNK), F32),
        compiler_params=_cparams(("parallel",)),
        name="gate_rows",
    )(gates_rows)


def _mlstm_kernel(q_ref, k_ref, v_ref, o_ref, z_ref, row_ref, ng_ref, out_ref,
                  mix_ref, coef_ref, den_ref, u_ref, nu_ref, c_ref, n_ref, *, nc):
    L = CHUNK
    scale = MLSTM_QK_DIM ** -0.5
    B_, D_, CM_, WE_, G_, ML_ = ROW_B, ROW_D, ROW_CM, ROW_WE, ROW_G, ROW_ML

    t_idx = lax.broadcasted_iota(jnp.int32, (L, L), 0)
    s_idx = lax.broadcasted_iota(jnp.int32, (L, L), 1)
    lower = s_idx <= t_idx
    upper = s_idx >= t_idx
    eye = jnp.where(s_idx == t_idx, 1.0, 0.0).astype(BF16)

    ones8 = jnp.ones((8, L), BF16)

    def chunk_rows(c):
        return slice(c * L, (c + 1) * L)

    def intra_group(gi, carry):
        cs = [gi * group + j for j in range(group)]
        rows = [chunk_rows(c) for c in cs]
        kts = [jnp.transpose(k_ref[r, :].astype(F32)).astype(BF16) for r in rows]
        scores = [jnp.dot(q_ref[r, :], kt, preferred_element_type=F32) for r, kt in zip(rows, kts)]
        for c, r, kt in zip(cs, rows, kts):
            ktf = kt.astype(F32)
            we = [row_ref[dr, WE_, pl.ds(c, 1), :] for dr in (FWD, BWD)]
            kts2 = jnp.concatenate([(ktf * w).astype(BF16) for w in we], axis=0)
            u2 = jnp.dot(kts2, v_ref[r, :], preferred_element_type=F32)
            we16 = jnp.concatenate([jnp.broadcast_to(w, (8, L)) for w in we], axis=0).astype(BF16)
            nu2 = jnp.dot(we16, k_ref[r, :], preferred_element_type=F32)
            for dr in (FWD, BWD):
                u_ref[dr, c] = u2[dr * MLSTM_QK_DIM:(dr + 1) * MLSTM_QK_DIM, :]
                nu_ref[dr, c] = nu2[dr * 8:(dr + 1) * 8, :]
        qks = []
        for c, s in zip(cs, scores):
            pair_qk = []
            for dr in (FWD, BWD):
                drow = row_ref[dr, D_, pl.ds(c, 1), :]
                cm_col = _to_columns(row_ref[dr, CM_, pl.ds(c, 1), :], L)
                w = jnp.exp(jnp.where(upper if dr == BWD else lower, drow - cm_col, -jnp.inf))
                pair_qk.append((s * w).astype(BF16))
            qks.append(jnp.concatenate(pair_qk, axis=0))
        for c, r, qk2 in zip(cs, rows, qks):
            num2 = jnp.dot(qk2, v_ref[r, :], preferred_element_type=F32).astype(BF16)
            den2 = lax.dot_general(ones8, qk2, _NT, preferred_element_type=F32)
            for dr in (FWD, BWD):
                mix_ref[c, 2 * dr * L:(2 * dr + 1) * L, :] = num2[dr * L:(dr + 1) * L, :]
                den_ref[dr, c] = den2[:, dr * L:(dr + 1) * L]
        return carry
    group = math.gcd(nc, 2)
    for gi in range(nc // group):
        intra_group(gi, 0)

    c_ref[...] = jnp.zeros_like(c_ref)
    n_ref[...] = jnp.zeros_like(n_ref)

    def inter_group(gi, carry):
        m = list(carry)
        chains = []
        for j in range(pair):
            for dr in (FWD, BWD):
                c = gi * pair + j if dr == FWD else nc - 1 - (gi * pair + j)
                brow = row_ref[dr, B_, pl.ds(c, 1), :]
                inter_ = brow + m[dr]
                amax = brow + row_ref[dr, CM_, pl.ds(c, 1), :]
                m_t = jnp.maximum(inter_, amax)
                al = jnp.exp(amax - m_t) * scale
                be = jnp.exp(inter_ - m_t) * scale
                floor = jnp.exp(-m_t)
                g = row_ref[dr, G_, pl.ds(c, 1), :]
                ml = row_ref[dr, ML_, pl.ds(c, 1), :]
                m_new = jnp.maximum(g + m[dr], ml)
                sc = jnp.exp(g + m[dr] - m_new)
                su = jnp.exp(ml - m_new)
                c_old = c_ref[dr]
                n_old = n_ref[dr]
                c_ref[dr] = (jnp.concatenate([sc, sc], axis=1) * c_old
                             + jnp.concatenate([su, su], axis=1) * u_ref[dr, c])
                n_ref[dr] = sc * n_old + su * nu_ref[dr, c]
                m[dr] = m_new
                chains.append((dr, c, al, be, floor, c_old.astype(BF16), n_old.astype(BF16)))
        prods = []
        for dr, c, al, be, floor, c_bf, n_bf in chains:
            qc = q_ref[chunk_rows(c), :]
            qn = lax.dot_general(n_bf, qc, _NT, preferred_element_type=F32)[0:1]
            prods.append((qn, jnp.dot(qc, c_bf, preferred_element_type=F32)))
        for (dr, c, al, be, floor, _, _), (qn, q_c) in zip(chains, prods):
            den = al * den_ref[dr, c][0:1] + be * qn
            r = 1.0 / jnp.maximum(jnp.abs(den), floor)
            coef_ref[c, 2 * dr:2 * dr + 1, :] = al * r
            coef_ref[c, 2 * dr + 1:2 * dr + 2, :] = be * r
            mix_ref[c, (2 * dr + 1) * L:(2 * dr + 2) * L, :] = q_c.astype(BF16)
        return tuple(m)

    pair = math.gcd(nc, 4)
    zero = jnp.zeros((1, L), F32)
    m_carry = (zero, zero)
    for gi in range(nc // pair):
        m_carry = inter_group(gi, m_carry)

    def fin_group(gi, carry):
        cs = [gi * fin + j for j in range(fin)]
        hs = []
        for c in cs:
            diags = [eye * jnp.broadcast_to(coef_ref[c, x:x + 1, :].astype(BF16), (L, L)) for x in range(4)]
            hs.append(jnp.dot(jnp.concatenate(diags, axis=1), mix_ref[c], preferred_element_type=F32))
        for c, h in zip(cs, hs):
            rows = chunk_rows(c)
            hm = h * _sigmoid(o_ref[rows, :].astype(F32))
            ms = jnp.mean(hm * hm, axis=-1, keepdims=True)
            hn = hm * lax.rsqrt(ms + EPS) * ng_ref[...]
            z = z_ref[rows, :].astype(F32)
            out_ref[rows, :] = (hn * _silu(z)).astype(BF16)
        return carry
    fin = math.gcd(nc, 4)
    for gi in range(nc // fin):
        fin_group(gi, 0)


def _mlstm(proj, rows, norm_gain, *, B, S):
    T = B * S
    nc = S // CHUNK
    dk, dv = MLSTM_QK_DIM, MLSTM_V_DIM
    return pl.pallas_call(
        functools.partial(_mlstm_kernel, nc=nc),
        grid=(B, MLSTM_HEADS),
        in_specs=[
            pl.BlockSpec((S, dk), lambda b, h: (b, OFF_MQ // dk + h)),
            pl.BlockSpec((S, dk), lambda b, h: (b, OFF_MK // dk + h)),
            pl.BlockSpec((S, dv), lambda b, h: (b, OFF_MV // dv + h)),
            pl.BlockSpec((S, dv), lambda b, h: (b, OFF_MO // dv + h)),
            pl.BlockSpec((S, dv), lambda b, h: (b, OFF_MZ // dv + h)),
            pl.BlockSpec((2, N_ROW_KINDS, nc, CHUNK), lambda b, h: (0, 0, h * B + b, 0)),
            pl.BlockSpec((1, dv), lambda b, h: (0, h)),
        ],
        out_specs=pl.BlockSpec((S, dv), lambda b, h: (b, h)),
        out_shape=jax.ShapeDtypeStruct((T, MLSTM_WIDTH), BF16),
        scratch_shapes=[
            pltpu.VMEM((nc, 4 * CHUNK, dv), BF16),
            pltpu.VMEM((nc, 8, CHUNK), F32),
            pltpu.VMEM((2, nc, 8, CHUNK), F32),
            pltpu.VMEM((2, nc, dk, dv), F32),
            pltpu.VMEM((2, nc, 8, dk), F32),
            pltpu.VMEM((2, dk, dv), F32),
            pltpu.VMEM((2, 8, dk), F32),
        ],
        compiler_params=_cparams(("parallel", "arbitrary")),
        name="mlstm",
    )(proj, proj, proj, proj, proj, rows, norm_gain)


def _outproj_kernel(a_ref, m_ref, wa_ref, wm_ref, x_ref, y_ref):
    y = jnp.dot(a_ref[...], wa_ref[...], preferred_element_type=F32)
    y = y + jnp.dot(m_ref[...], wm_ref[...], preferred_element_type=F32)
    y_ref[...] = x_ref[...] + y


def _out_projection(attn, mlstm, w_out, x2, *, tm):
    T, D = x2.shape
    return pl.pallas_call(
        _outproj_kernel,
        grid=(T // tm,),
        in_specs=[
            pl.BlockSpec((tm, ATTN_WIDTH), lambda i: (i, 0)),
            pl.BlockSpec((tm, MLSTM_WIDTH), lambda i: (i, 0)),
            pl.BlockSpec((ATTN_WIDTH, D), lambda i: (0, 0)),
            pl.BlockSpec((MLSTM_WIDTH, D), lambda i: (1, 0)),
            pl.BlockSpec((tm, D), lambda i: (i, 0)),
        ],
        out_specs=pl.BlockSpec((tm, D), lambda i: (i, 0)),
        out_shape=jax.ShapeDtypeStruct((T, D), F32),
        compiler_params=_cparams(("parallel",)),
        name="outproj",
    )(attn, mlstm, w_out, w_out, x2)


def _rope_tables(S):
    row = (jnp.arange(S) // GRID_W).astype(F32)
    col = (jnp.arange(S) % GRID_W).astype(F32)
    nf = HEAD_DIM // 4
    inv = 1.0 / (ROPE_THETA ** (jnp.arange(nf, dtype=F32) / nf))
    ang_r = row[:, None] * inv
    ang_c = col[:, None] * inv
    ang = jnp.concatenate([ang_r, ang_r, ang_c, ang_c], axis=-1)
    cos, sin = jnp.cos(ang), jnp.sin(ang)
    quarter = (jnp.arange(HEAD_DIM) // nf) % 2
    sin_a = jnp.where(quarter == 0, -sin, 0.0)
    sin_b = jnp.where(quarter == 1, sin, 0.0)
    return cos, sin_a, sin_b


def _cast_t_kernel(wt_ref, o_ref):
    for r in range(wt_ref.shape[0] // LANES):
        rows = slice(r * LANES, (r + 1) * LANES)
        o_ref[:, rows] = jnp.transpose(wt_ref[rows, :]).astype(BF16)


def _cast_transposed(w_t, *, n_cols, block):
    D = w_t.shape[1]
    return pl.pallas_call(
        _cast_t_kernel,
        grid=(n_cols // block,),
        in_specs=[pl.BlockSpec((block, D), lambda j: (j, 0))],
        out_specs=pl.BlockSpec((D, block), lambda j: (0, j)),
        out_shape=jax.ShapeDtypeStruct((D, n_cols), BF16),
        compiler_params=_cparams(("parallel",)),
        name="cast_weights",
    )(w_t)


def _layer(x, norm_g, w_in, b_gates, q_norm_g, k_norm_g, mlstm_norm_g, w_out):
    B, S, D = x.shape
    T = B * S
    nc = S // CHUNK
    x2 = x.reshape(T, D)

    w_in_t = w_in.T
    w_main = _cast_transposed(w_in_t, n_cols=N_MAIN, block=512)
    w_gates = w_in_t[N_MAIN:].astype(BF16)
    tm = math.gcd(S, 256)
    bias = b_gates.astype(F32).reshape(N_GATE_COLS, 1)
    cos, sin_a, sin_b = _rope_tables(S)
    proj, gates_t = _in_projection(x2, norm_g.reshape(1, D), w_main, w_gates, bias, cos, sin_a, sin_b,
                                   q_norm_g.reshape(1, HEAD_DIM), k_norm_g.reshape(1, HEAD_DIM),
                                   tm=tm, tn=512, seq_len=S)

    attn = _attention(proj, B=B, S=S, tq=math.gcd(S, 256), tiles_per_step=4 if S % 1024 == 0 else 1)

    gates_rows = gates_t.reshape(4, MLSTM_HEADS * B * nc, CHUNK)
    mlstm = _mlstm(proj, _gate_rows(gates_rows), mlstm_norm_g.reshape(1, MLSTM_WIDTH), B=B, S=S)

    y = _out_projection(attn, mlstm, w_out.astype(BF16), x2, tm=math.gcd(T, 512))
    return y.reshape(B, S, D)


def kernel(x_prompt, x_sample, norm_g, w_in, b_gates, q_norm_g, k_norm_g, mlstm_norm_g, w_out):
    depth = norm_g.shape[0]
    outs = []
    for x in (x_prompt, x_sample):
        for l in range(depth):
            x = _layer(x, norm_g[l], w_in[l], b_gates[l], q_norm_g[l], k_norm_g[l],
                       mlstm_norm_g[l], w_out[l])
        outs.append(x)
    return tuple(outs)
```

```python
import functools
import math

import jax
import jax.numpy as jnp
from jax import lax
from jax.experimental import pallas as pl
from jax.experimental.pallas import tpu as pltpu

F32 = jnp.float32
BF16 = jnp.bfloat16

ATTN_HEADS = 8
ATTN_KV_HEADS = 2
HEAD_DIM = 128
ATTN_WIDTH = ATTN_HEADS * HEAD_DIM
KV_WIDTH = ATTN_KV_HEADS * HEAD_DIM
GROUP = ATTN_HEADS // ATTN_KV_HEADS
MLSTM_HEADS = 4
MLSTM_QK_DIM = 128
MLSTM_V_DIM = 256
MLSTM_QK_WIDTH = MLSTM_HEADS * MLSTM_QK_DIM
MLSTM_WIDTH = MLSTM_HEADS * MLSTM_V_DIM
N_GATE_COLS = 4 * MLSTM_HEADS
GRID_W = 64
ROPE_THETA = 10000.0
CHUNK = 128
EPS = 1e-6

OFF_AQ = 0
OFF_AK = OFF_AQ + ATTN_WIDTH
OFF_AV = OFF_AK + KV_WIDTH
OFF_AZ = OFF_AV + KV_WIDTH
OFF_MQ = OFF_AZ + ATTN_WIDTH
OFF_MK = OFF_MQ + MLSTM_QK_WIDTH
OFF_MV = OFF_MK + MLSTM_QK_WIDTH
OFF_MO = OFF_MV + MLSTM_WIDTH
OFF_MZ = OFF_MO + MLSTM_WIDTH
OFF_GATES = OFF_MZ + MLSTM_WIDTH
N_MAIN = OFF_GATES

LANES = 128
VMEM_LIMIT = 56 * 1024 * 1024


def _cparams(sem):
    return pltpu.CompilerParams(dimension_semantics=sem, vmem_limit_bytes=VMEM_LIMIT)


_NT = (((1,), (1,)), ((), ()))


def _sigmoid(x):
    return 0.5 * jnp.tanh(0.5 * x) + 0.5


def _silu(x):
    hx = 0.5 * x
    return hx * jnp.tanh(hx) + hx


def _norm_rope(x, gain, cos, sin_a, sin_b):
    ms = jnp.mean(x * x, axis=-1, keepdims=True)
    xn = x * lax.rsqrt(ms + EPS) * gain
    up = pltpu.roll(xn, 3 * HEAD_DIM // 4, axis=1)
    down = pltpu.roll(xn, HEAD_DIM // 4, axis=1)
    return xn * cos + up * sin_a + down * sin_b


def _inproj_kernel(x_ref, g_ref, w_ref, wg_ref, bg_ref, cos_ref, sa_ref, sb_ref, qg_ref, kg_ref,
                   o_ref, og_ref, h_ref, *, tn, row_chunk, seq_len):
    s = pl.program_id(0)
    tm, _ = x_ref.shape
    n_tiles = w_ref.shape[1] // tn
    n_chunks = tm // row_chunk
    nxt = s % 2
    cur = 1 - nxt

    @pl.when(s == 0)
    def _():
        h_ref[1] = jnp.zeros(h_ref.shape[1:], BF16)

    def normalise(r):
        rows = slice(r * row_chunk, (r + 1) * row_chunk)
        x = x_ref[rows, :]
        ms = jnp.mean(x * x, axis=-1, keepdims=True)
        h_ref[nxt, rows, :] = (x * lax.rsqrt(ms + EPS) * g_ref[...]).astype(BF16)

    pos = pl.ds(pl.multiple_of((jnp.maximum(s - 1, 0) * tm) % seq_len, tm), tm)
    qscale = math.log2(math.e) / math.sqrt(HEAD_DIM)

    first_plain = -(-OFF_AZ // tn)
    for j in range(n_tiles):
        if j == n_tiles // 2:
            og_ref[...] = (lax.dot_general(wg_ref[...], h_ref[cur], _NT, preferred_element_type=F32)
                           + bg_ref[...])
        acc = jnp.dot(h_ref[cur], w_ref[:, j * tn:(j + 1) * tn], preferred_element_type=F32)
        for c0 in range(j * tn, (j + 1) * tn, HEAD_DIM):
            piece = acc[:, c0 - j * tn:c0 - j * tn + HEAD_DIM]
            if c0 < OFF_AK:
                piece = _norm_rope(piece, qg_ref[...], cos_ref[pos, :], sa_ref[pos, :], sb_ref[pos, :]) * qscale
            elif c0 < OFF_AV:
                piece = _norm_rope(piece, kg_ref[...], cos_ref[pos, :], sa_ref[pos, :], sb_ref[pos, :])
            o_ref[:, c0:c0 + HEAD_DIM] = piece.astype(BF16)
        if 0 <= j - first_plain < n_chunks:
            normalise(j - first_plain)
    for r in range(max(n_tiles - first_plain, 0), n_chunks):
        normalise(r)


def _in_projection(x2, norm_g, w_main, w_gates, b_gates, cos, sin_a, sin_b, q_gain, k_gain, *, tm, tn, seq_len):
    T, D = x2.shape
    N = N_MAIN
    n_blocks = T // tm
    resident = pl.Buffered(1)
    const = lambda s: (0, 0)
    return pl.pallas_call(
        functools.partial(_inproj_kernel, tn=tn, row_chunk=32, seq_len=seq_len),
        grid=(n_blocks + 1,),
        in_specs=[
            pl.BlockSpec((tm, D), lambda s: (jnp.minimum(s, n_blocks - 1), 0)),
            pl.BlockSpec((1, D), const),
            pl.BlockSpec((D, N_MAIN), const, pipeline_mode=resident),
            pl.BlockSpec((N_GATE_COLS, D), const),
            pl.BlockSpec((N_GATE_COLS, 1), const),
            pl.BlockSpec((seq_len, HEAD_DIM), const, pipeline_mode=resident),
            pl.BlockSpec((seq_len, HEAD_DIM), const, pipeline_mode=resident),
            pl.BlockSpec((seq_len, HEAD_DIM), const, pipeline_mode=resident),
            pl.BlockSpec((1, HEAD_DIM), const),
            pl.BlockSpec((1, HEAD_DIM), const),
        ],
        out_specs=[
            pl.BlockSpec((tm, N), lambda s: (jnp.maximum(s - 1, 0), 0)),
            pl.BlockSpec((N_GATE_COLS, tm), lambda s: (0, jnp.maximum(s - 1, 0))),
        ],
        out_shape=[
            jax.ShapeDtypeStruct((T, N), BF16),
            jax.ShapeDtypeStruct((N_GATE_COLS, T), F32),
        ],
        scratch_shapes=[pltpu.VMEM((2, tm, D), BF16)],
        compiler_params=_cparams(("arbitrary",)),
        name="inproj",
    )(x2, norm_g, w_main, w_gates, b_gates, cos, sin_a, sin_b, q_gain, k_gain)


def _attn_kernel(q_ref, k_ref, v_ref, z_ref, o_ref, vx_ref, *, tq):
    qi = pl.program_id(2)

    @pl.when(qi == 0)
    def _():
        vx_ref[:, :HEAD_DIM] = v_ref[...]
        vx_ref[:, HEAD_DIM:] = jnp.ones(v_ref.shape, BF16)

    block_rows = q_ref.shape[0]
    units = [(t, g) for t in range(block_rows // tq) for g in range(GROUP)]

    def scores(unit):
        t, g = unit
        q = q_ref[t * tq:(t + 1) * tq, g * HEAD_DIM:(g + 1) * HEAD_DIM]
        return lax.dot_general(q, k_ref[...], _NT, preferred_element_type=F32)

    s_next = scores(units[0])
    for i, (t, g) in enumerate(units):
        rows = slice(t * tq, (t + 1) * tq)
        cols = slice(g * HEAD_DIM, (g + 1) * HEAD_DIM)
        s = s_next
        if i + 1 < len(units):
            s_next = scores(units[i + 1])
        m = jnp.max(s, axis=-1, keepdims=True)
        p = jnp.exp2(s - m).astype(BF16)
        ox = jnp.dot(p, vx_ref[...], preferred_element_type=F32)
        o = ox[:, :HEAD_DIM] / ox[:, HEAD_DIM:]
        z = z_ref[rows, cols].astype(F32)
        o_ref[rows, cols] = (o * _silu(z)).astype(BF16)


def _attention(proj, *, B, S, tq, tiles_per_step):
    T = B * S
    bq = tq * tiles_per_step
    nq = S // bq
    gw = GROUP * HEAD_DIM
    return pl.pallas_call(
        functools.partial(_attn_kernel, tq=tq),
        grid=(B, ATTN_KV_HEADS, nq),
        in_specs=[
            pl.BlockSpec((bq, gw), lambda b, kv, qi: (b * nq + qi, OFF_AQ // gw + kv)),
            pl.BlockSpec((S, HEAD_DIM), lambda b, kv, qi: (b, OFF_AK // HEAD_DIM + kv)),
            pl.BlockSpec((S, HEAD_DIM), lambda b, kv, qi: (b, OFF_AV // HEAD_DIM + kv)),
            pl.BlockSpec((bq, gw), lambda b, kv, qi: (b * nq + qi, OFF_AZ // gw + kv)),
        ],
        out_specs=pl.BlockSpec((bq, gw), lambda b, kv, qi: (b * nq + qi, kv)),
        out_shape=jax.ShapeDtypeStruct((T, ATTN_WIDTH), BF16),
        scratch_shapes=[pltpu.VMEM((S, 2 * HEAD_DIM), BF16)],
        compiler_params=_cparams(("parallel", "parallel", "arbitrary")),
        name="attention",
    )(proj, proj, proj, proj)


def _split3(x):
    hi = x.astype(BF16)
    r1 = x - hi.astype(F32)
    mid = r1.astype(BF16)
    lo = (r1 - mid.astype(F32)).astype(BF16)
    return hi, mid, lo


def _log_sigmoid(x):
    return jnp.minimum(x, 0.0) - jnp.log1p(jnp.exp(-jnp.abs(x)))


def _cummax_lanes(x, reverse):
    n = x.shape[-1]
    lane = lax.broadcasted_iota(jnp.int32, x.shape, 1)
    k = 1
    while k < n:
        if reverse:
            shifted = jnp.where(lane < n - k, pltpu.roll(x, n - k, axis=1), -jnp.inf)
        else:
            shifted = jnp.where(lane >= k, pltpu.roll(x, k, axis=1), -jnp.inf)
        x = jnp.maximum(x, shifted)
        k *= 2
    return x


def _to_columns(row, n):
    return jnp.transpose(jnp.broadcast_to(row, (n, n)))


FWD, BWD = 0, 1
ROW_B, ROW_D, ROW_CM, ROW_WE, ROW_G, ROW_ML = range(6)
N_ROW_KINDS = 6


def _gate_rows_kernel(g_ref, out_ref):
    L = CHUNK
    n = g_ref.shape[1]
    t_idx = lax.broadcasted_iota(jnp.int32, (L, L), 0)
    s_idx = lax.broadcasted_iota(jnp.int32, (L, L), 1)
    for dr in (FWD, BWD):
        i_rows = g_ref[2 * dr]
        lf = _log_sigmoid(g_ref[2 * dr + 1])
        tri = jnp.where(t_idx >= s_idx if dr == BWD else t_idx <= s_idx, 1.0, 0.0).astype(BF16)
        b = sum(jnp.dot(p, tri, preferred_element_type=F32) for p in _split3(lf))
        d = i_rows - b
        g = b[:, 0:1] if dr == BWD else b[:, L - 1:L]
        e = g + d
        mloc = jnp.max(e, axis=-1, keepdims=True)
        out_ref[dr, ROW_B] = b
        out_ref[dr, ROW_D] = d
        out_ref[dr, ROW_CM] = _cummax_lanes(d, reverse=(dr == BWD))
        out_ref[dr, ROW_WE] = jnp.exp(e - mloc)
        out_ref[dr, ROW_G] = jnp.broadcast_to(g, (n, L))
        out_ref[dr, ROW_ML] = jnp.broadcast_to(mloc, (n, L))


def _gate_rows(gates_rows):
    n_rows = gates_rows.shape[1]
    rb = math.gcd(n_rows, 512)
    return pl.pallas_call(
        _gate_rows_kernel,
        grid=(n_rows // rb,),
        in_specs=[pl.BlockSpec((4, rb, CHUNK), lambda i: (0, i, 0))],
        out_specs=pl.BlockSpec((2, N_ROW_KINDS, rb, CHUNK), lambda i: (0, 0, i, 0)),
        out_shape=jax.ShapeDtypeStruct((2, N_ROW_KINDS, n_rows, CHUNK), F32),
        compiler_params=_cparams(("parallel",)),
        name="gate_rows",
    )(gates_rows)


def _mlstm_kernel(q_ref, k_ref, v_ref, o_ref, z_ref, row_ref, ng_ref, zero_ref, out_ref,
                  g1_ref, g2_ref, mix_ref, coef_ref, den_ref, u_ref, nu_ref, c_ref, n_ref, *, nc):
    L = CHUNK
    scale = MLSTM_QK_DIM ** -0.5
    B_, D_, CM_, WE_, G_, ML_ = ROW_B, ROW_D, ROW_CM, ROW_WE, ROW_G, ROW_ML

    t_idx = lax.broadcasted_iota(jnp.int32, (L, L), 0)
    s_idx = lax.broadcasted_iota(jnp.int32, (L, L), 1)
    lower = s_idx <= t_idx
    upper = s_idx >= t_idx
    eye = jnp.where(s_idx == t_idx, 1.0, 0.0).astype(BF16)

    ones8 = jnp.ones((8, L), BF16)

    def chunk_rows(c):
        return slice(c * L, (c + 1) * L)

    def intra_group(gi, carry):
        cs = [gi * group + j for j in range(group)]
        rows = [chunk_rows(c) for c in cs]
        kts = [jnp.transpose(k_ref[r, :].astype(F32)).astype(BF16) for r in rows]
        scores = [jnp.dot(q_ref[r, :], kt, preferred_element_type=F32) for r, kt in zip(rows, kts)]
        tied = []
        for r in rows:
            g1 = _sigmoid(o_ref[r, :].astype(F32))
            g2 = ng_ref[...] * _silu(z_ref[r, :].astype(F32))
            g1_ref[r, :] = g1
            g2_ref[r, :] = g2
            bits = pltpu.bitcast(g1, jnp.int32) | pltpu.bitcast(g2, jnp.int32)
            bits = bits[:, :L] | bits[:, L:]
            acc = bits[0:8, :]
            for i in range(1, L // 8):
                acc = acc | bits[8 * i:8 * i + 8, :]
            tied.append((acc & zero_ref[...]).astype(F32)[0:1, :])
        for c, r, kt, tie in zip(cs, rows, kts, tied):
            ktf = kt.astype(F32)
            we = [row_ref[dr, WE_, pl.ds(c, 1), :] + tie for dr in (FWD, BWD)]
            kts2 = jnp.concatenate([(ktf * w).astype(BF16) for w in we], axis=0)
            u2 = jnp.dot(kts2, v_ref[r, :], preferred_element_type=F32)
            we16 = jnp.concatenate([jnp.broadcast_to(w, (8, L)) for w in we], axis=0).astype(BF16)
            nu2 = jnp.dot(we16, k_ref[r, :], preferred_element_type=F32)
            for dr in (FWD, BWD):
                u_ref[dr, c] = u2[dr * MLSTM_QK_DIM:(dr + 1) * MLSTM_QK_DIM, :]
                nu_ref[dr, c] = nu2[dr * 8:(dr + 1) * 8, :]
        qks = []
        for c, s in zip(cs, scores):
            pair_qk = []
            for dr in (FWD, BWD):
                drow = row_ref[dr, D_, pl.ds(c, 1), :]
                cm_col = _to_columns(row_ref[dr, CM_, pl.ds(c, 1), :], L)
                w = jnp.exp(jnp.where(upper if dr == BWD else lower, drow - cm_col, -jnp.inf))
                pair_qk.append((s * w).astype(BF16))
            qks.append(jnp.concatenate(pair_qk, axis=0))
        for c, r, qk2 in zip(cs, rows, qks):
            num2 = jnp.dot(qk2, v_ref[r, :], preferred_element_type=F32).astype(BF16)
            den2 = lax.dot_general(ones8, qk2, _NT, preferred_element_type=F32)
            for dr in (FWD, BWD):
                mix_ref[c, 2 * dr * L:(2 * dr + 1) * L, :] = num2[dr * L:(dr + 1) * L, :]
                den_ref[dr, c] = den2[:, dr * L:(dr + 1) * L]
        return carry
    group = math.gcd(nc, 2)
    for gi in range(nc // group):
        intra_group(gi, 0)

    c_ref[...] = jnp.zeros_like(c_ref)
    n_ref[...] = jnp.zeros_like(n_ref)

    def inter_group(gi, carry):
        m = list(carry)
        chains = []
        for j in range(pair):
            for dr in (FWD, BWD):
                c = gi * pair + j if dr == FWD else nc - 1 - (gi * pair + j)
                brow = row_ref[dr, B_, pl.ds(c, 1), :]
                inter_ = brow + m[dr]
                amax = brow + row_ref[dr, CM_, pl.ds(c, 1), :]
                m_t = jnp.maximum(inter_, amax)
                al = jnp.exp(amax - m_t) * scale
                be = jnp.exp(inter_ - m_t) * scale
                floor = jnp.exp(-m_t)
                g = row_ref[dr, G_, pl.ds(c, 1), :]
                ml = row_ref[dr, ML_, pl.ds(c, 1), :]
                m_new = jnp.maximum(g + m[dr], ml)
                sc = jnp.exp(g + m[dr] - m_new)
                su = jnp.exp(ml - m_new)
                c_old = c_ref[dr]
                n_old = n_ref[dr]
                c_ref[dr] = (jnp.concatenate([sc, sc], axis=1) * c_old
                             + jnp.concatenate([su, su], axis=1) * u_ref[dr, c])
                n_ref[dr] = sc * n_old + su * nu_ref[dr, c]
                m[dr] = m_new
                chains.append((dr, c, al, be, floor, c_old.astype(BF16), n_old.astype(BF16)))
        prods = []
        for dr, c, al, be, floor, c_bf, n_bf in chains:
            qc = q_ref[chunk_rows(c), :]
            qn = lax.dot_general(n_bf, qc, _NT, preferred_element_type=F32)[0:1]
            prods.append((qn, jnp.dot(qc, c_bf, preferred_element_type=F32)))
        for (dr, c, al, be, floor, _, _), (qn, q_c) in zip(chains, prods):
            den = al * den_ref[dr, c][0:1] + be * qn
            r = 1.0 / jnp.maximum(jnp.abs(den), floor)
            coef_ref[c, 2 * dr:2 * dr + 1, :] = al * r
            coef_ref[c, 2 * dr + 1:2 * dr + 2, :] = be * r
            mix_ref[c, (2 * dr + 1) * L:(2 * dr + 2) * L, :] = q_c.astype(BF16)
        return tuple(m)

    pair = math.gcd(nc, 4)
    zero = jnp.zeros((1, L), F32)
    m_carry = (zero, zero)
    for gi in range(nc // pair):
        m_carry = inter_group(gi, m_carry)

    def fin_group(gi, carry):
        cs = [gi * fin + j for j in range(fin)]
        hs = []
        for c in cs:
            diags = [eye * jnp.broadcast_to(coef_ref[c, x:x + 1, :].astype(BF16), (L, L)) for x in range(4)]
            hs.append(jnp.dot(jnp.concatenate(diags, axis=1), mix_ref[c], preferred_element_type=F32))
        for c, h in zip(cs, hs):
            rows = chunk_rows(c)
            hm = h * g1_ref[rows, :]
            ms = jnp.mean(hm * hm, axis=-1, keepdims=True)
            out_ref[rows, :] = (hm * lax.rsqrt(ms + EPS) * g2_ref[rows, :]).astype(BF16)
        return carry
    fin = math.gcd(nc, 4)
    for gi in range(nc // fin):
        fin_group(gi, 0)


def _mlstm(proj, rows, norm_gain, *, B, S):
    T = B * S
    nc = S // CHUNK
    dk, dv = MLSTM_QK_DIM, MLSTM_V_DIM
    return pl.pallas_call(
        functools.partial(_mlstm_kernel, nc=nc),
        grid=(B, MLSTM_HEADS),
        in_specs=[
            pl.BlockSpec((S, dk), lambda b, h: (b, OFF_MQ // dk + h)),
            pl.BlockSpec((S, dk), lambda b, h: (b, OFF_MK // dk + h)),
            pl.BlockSpec((S, dv), lambda b, h: (b, OFF_MV // dv + h)),
            pl.BlockSpec((S, dv), lambda b, h: (b, OFF_MO // dv + h)),
            pl.BlockSpec((S, dv), lambda b, h: (b, OFF_MZ // dv + h)),
            pl.BlockSpec((2, N_ROW_KINDS, nc, CHUNK), lambda b, h: (0, 0, h * B + b, 0)),
            pl.BlockSpec((1, dv), lambda b, h: (0, h)),
            pl.BlockSpec((8, CHUNK), lambda b, h: (0, 0)),
        ],
        out_specs=pl.BlockSpec((S, dv), lambda b, h: (b, h)),
        out_shape=jax.ShapeDtypeStruct((T, MLSTM_WIDTH), BF16),
        scratch_shapes=[
            pltpu.VMEM((S, dv), F32),
            pltpu.VMEM((S, dv), F32),
            pltpu.VMEM((nc, 4 * CHUNK, dv), BF16),
            pltpu.VMEM((nc, 8, CHUNK), F32),
            pltpu.VMEM((2, nc, 8, CHUNK), F32),
            pltpu.VMEM((2, nc, dk, dv), F32),
            pltpu.VMEM((2, nc, 8, dk), F32),
            pltpu.VMEM((2, dk, dv), F32),
            pltpu.VMEM((2, 8, dk), F32),
        ],
        compiler_params=_cparams(("parallel", "arbitrary")),
        name="mlstm",
    )(proj, proj, proj, proj, proj, rows, norm_gain, jnp.zeros((8, CHUNK), jnp.int32))


def _outproj_kernel(a_ref, m_ref, wa_ref, wm_ref, x_ref, y_ref):
    y = jnp.dot(a_ref[...], wa_ref[...], preferred_element_type=F32)
    y = y + jnp.dot(m_ref[...], wm_ref[...], preferred_element_type=F32)
    y_ref[...] = x_ref[...] + y


def _out_projection(attn, mlstm, w_out, x2, *, tm):
    T, D = x2.shape
    return pl.pallas_call(
        _outproj_kernel,
        grid=(T // tm,),
        in_specs=[
            pl.BlockSpec((tm, ATTN_WIDTH), lambda i: (i, 0)),
            pl.BlockSpec((tm, MLSTM_WIDTH), lambda i: (i, 0)),
            pl.BlockSpec((ATTN_WIDTH, D), lambda i: (0, 0)),
            pl.BlockSpec((MLSTM_WIDTH, D), lambda i: (1, 0)),
            pl.BlockSpec((tm, D), lambda i: (i, 0)),
        ],
        out_specs=pl.BlockSpec((tm, D), lambda i: (i, 0)),
        out_shape=jax.ShapeDtypeStruct((T, D), F32),
        compiler_params=_cparams(("parallel",)),
        name="outproj",
    )(attn, mlstm, w_out, w_out, x2)


def _rope_tables(S):
    row = (jnp.arange(S) // GRID_W).astype(F32)
    col = (jnp.arange(S) % GRID_W).astype(F32)
    nf = HEAD_DIM // 4
    inv = 1.0 / (ROPE_THETA ** (jnp.arange(nf, dtype=F32) / nf))
    ang_r = row[:, None] * inv
    ang_c = col[:, None] * inv
    ang = jnp.concatenate([ang_r, ang_r, ang_c, ang_c], axis=-1)
    cos, sin = jnp.cos(ang), jnp.sin(ang)
    quarter = (jnp.arange(HEAD_DIM) // nf) % 2
    sin_a = jnp.where(quarter == 0, -sin, 0.0)
    sin_b = jnp.where(quarter == 1, sin, 0.0)
    return cos, sin_a, sin_b


def _cast_t_kernel(wt_ref, o_ref):
    for r in range(wt_ref.shape[0] // LANES):
        rows = slice(r * LANES, (r + 1) * LANES)
        o_ref[:, rows] = jnp.transpose(wt_ref[rows, :]).astype(BF16)


def _cast_transposed(w_t, *, n_cols, block):
    D = w_t.shape[1]
    return pl.pallas_call(
        _cast_t_kernel,
        grid=(n_cols // block,),
        in_specs=[pl.BlockSpec((block, D), lambda j: (j, 0))],
        out_specs=pl.BlockSpec((D, block), lambda j: (0, j)),
        out_shape=jax.ShapeDtypeStruct((D, n_cols), BF16),
        compiler_params=_cparams(("parallel",)),
        name="cast_weights",
    )(w_t)


def _layer(x, norm_g, w_in, b_gates, q_norm_g, k_norm_g, mlstm_norm_g, w_out):
    B, S, D = x.shape
    T = B * S
    nc = S // CHUNK
    x2 = x.reshape(T, D)

    w_in_t = w_in.T
    w_main = _cast_transposed(w_in_t, n_cols=N_MAIN, block=512)
    w_gates = w_in_t[N_MAIN:].astype(BF16)
    tm = math.gcd(S, 256)
    bias = b_gates.astype(F32).reshape(N_GATE_COLS, 1)
    cos, sin_a, sin_b = _rope_tables(S)
    proj, gates_t = _in_projection(x2, norm_g.reshape(1, D), w_main, w_gates, bias, cos, sin_a, sin_b,
                                   q_norm_g.reshape(1, HEAD_DIM), k_norm_g.reshape(1, HEAD_DIM),
                                   tm=tm, tn=512, seq_len=S)

    attn = _attention(proj, B=B, S=S, tq=math.gcd(S, 256), tiles_per_step=8 if S % 2048 == 0 else 1)

    gates_rows = gates_t.reshape(4, MLSTM_HEADS * B * nc, CHUNK)
    mlstm = _mlstm(proj, _gate_rows(gates_rows), mlstm_norm_g.reshape(1, MLSTM_WIDTH), B=B, S=S)

    y = _out_projection(attn, mlstm, w_out.astype(BF16), x2, tm=math.gcd(T, 512))
    return y.reshape(B, S, D)


def kernel(x_prompt, x_sample, norm_g, w_in, b_gates, q_norm_g, k_norm_g, mlstm_norm_g, w_out):
    depth = norm_g.shape[0]
    outs = []
    for x in (x_prompt, x_sample):
        for l in range(depth):
            x = _layer(x, norm_g[l], w_in[l], b_gates[l], q_norm_g[l], k_norm_g[l],
                       mlstm_norm_g[l], w_out[l])
        outs.append(x)
    return tuple(outs)
```

```python
import functools
import math

import jax
import jax.numpy as jnp
from jax import lax
from jax.experimental import pallas as pl
from jax.experimental.pallas import tpu as pltpu

F32 = jnp.float32
BF16 = jnp.bfloat16

ATTN_HEADS = 8
ATTN_KV_HEADS = 2
HEAD_DIM = 128
ATTN_WIDTH = ATTN_HEADS * HEAD_DIM
KV_WIDTH = ATTN_KV_HEADS * HEAD_DIM
GROUP = ATTN_HEADS // ATTN_KV_HEADS
MLSTM_HEADS = 4
MLSTM_QK_DIM = 128
MLSTM_V_DIM = 256
MLSTM_QK_WIDTH = MLSTM_HEADS * MLSTM_QK_DIM
MLSTM_WIDTH = MLSTM_HEADS * MLSTM_V_DIM
N_GATE_COLS = 4 * MLSTM_HEADS
GRID_W = 64
ROPE_THETA = 10000.0
CHUNK = 128
EPS = 1e-6

OFF_AQ = 0
OFF_AK = OFF_AQ + ATTN_WIDTH
OFF_AV = OFF_AK + KV_WIDTH
OFF_AZ = OFF_AV + KV_WIDTH
OFF_MQ = OFF_AZ + ATTN_WIDTH
OFF_MK = OFF_MQ + MLSTM_QK_WIDTH
OFF_MV = OFF_MK + MLSTM_QK_WIDTH
OFF_MO = OFF_MV + MLSTM_WIDTH
OFF_MZ = OFF_MO + MLSTM_WIDTH
OFF_GATES = OFF_MZ + MLSTM_WIDTH
N_MAIN = OFF_GATES

LANES = 128
VMEM_LIMIT = 56 * 1024 * 1024


def _cparams(sem):
    return pltpu.CompilerParams(dimension_semantics=sem, vmem_limit_bytes=VMEM_LIMIT)


_NT = (((1,), (1,)), ((), ()))


def _sigmoid(x):
    return 0.5 * jnp.tanh(0.5 * x) + 0.5


def _silu(x):
    hx = 0.5 * x
    return hx * jnp.tanh(hx) + hx


def _norm_rope(x, gain, cos, sin_a, sin_b):
    ms = jnp.mean(x * x, axis=-1, keepdims=True)
    xn = x * lax.rsqrt(ms + EPS) * gain
    up = pltpu.roll(xn, 3 * HEAD_DIM // 4, axis=1)
    down = pltpu.roll(xn, HEAD_DIM // 4, axis=1)
    return xn * cos + up * sin_a + down * sin_b


def _inproj_kernel(x_ref, g_ref, w_ref, wg_ref, bg_ref, cos_ref, sa_ref, sb_ref, qg_ref, kg_ref,
                   o_ref, og_ref, h_ref, *, tn, row_chunk, seq_len):
    s = pl.program_id(0)
    tm, _ = x_ref.shape
    n_tiles = w_ref.shape[1] // tn
    n_chunks = tm // row_chunk
    nxt = s % 2
    cur = 1 - nxt

    @pl.when(s == 0)
    def _():
        h_ref[1] = jnp.zeros(h_ref.shape[1:], BF16)

    def normalise(r):
        rows = slice(r * row_chunk, (r + 1) * row_chunk)
        x = x_ref[rows, :]
        ms = jnp.mean(x * x, axis=-1, keepdims=True)
        h_ref[nxt, rows, :] = (x * lax.rsqrt(ms + EPS) * g_ref[...]).astype(BF16)

    pos = pl.ds(pl.multiple_of((jnp.maximum(s - 1, 0) * tm) % seq_len, tm), tm)
    qscale = math.log2(math.e) / math.sqrt(HEAD_DIM)

    first_plain = -(-OFF_AZ // tn)
    per_tile = -(-n_chunks // (n_tiles - first_plain))
    for j in range(n_tiles):
        if j == n_tiles // 2:
            og_ref[...] = (lax.dot_general(wg_ref[...], h_ref[cur], _NT, preferred_element_type=F32)
                           + bg_ref[...])
        acc = jnp.dot(h_ref[cur], w_ref[:, j * tn:(j + 1) * tn], preferred_element_type=F32)
        for c0 in range(j * tn, (j + 1) * tn, HEAD_DIM):
            piece = acc[:, c0 - j * tn:c0 - j * tn + HEAD_DIM]
            if c0 < OFF_AK:
                piece = _norm_rope(piece, qg_ref[...], cos_ref[pos, :], sa_ref[pos, :], sb_ref[pos, :]) * qscale
            elif c0 < OFF_AV:
                piece = _norm_rope(piece, kg_ref[...], cos_ref[pos, :], sa_ref[pos, :], sb_ref[pos, :])
            o_ref[:, c0:c0 + HEAD_DIM] = piece.astype(BF16)
        if j >= first_plain:
            for r in range((j - first_plain) * per_tile, min((j - first_plain + 1) * per_tile, n_chunks)):
                normalise(r)


def _in_projection(x2, norm_g, w_main, w_gates, b_gates, cos, sin_a, sin_b, q_gain, k_gain, *, tm, tn, seq_len):
    T, D = x2.shape
    N = N_MAIN
    n_blocks = T // tm
    resident = pl.Buffered(1)
    const = lambda s: (0, 0)
    return pl.pallas_call(
        functools.partial(_inproj_kernel, tn=tn, row_chunk=32, seq_len=seq_len),
        grid=(n_blocks + 1,),
        in_specs=[
            pl.BlockSpec((tm, D), lambda s: (jnp.minimum(s, n_blocks - 1), 0)),
            pl.BlockSpec((1, D), const),
            pl.BlockSpec((D, N_MAIN), const, pipeline_mode=resident),
            pl.BlockSpec((N_GATE_COLS, D), const),
            pl.BlockSpec((N_GATE_COLS, 1), const),
            pl.BlockSpec((seq_len, HEAD_DIM), const, pipeline_mode=resident),
            pl.BlockSpec((seq_len, HEAD_DIM), const, pipeline_mode=resident),
            pl.BlockSpec((seq_len, HEAD_DIM), const, pipeline_mode=resident),
            pl.BlockSpec((1, HEAD_DIM), const),
            pl.BlockSpec((1, HEAD_DIM), const),
        ],
        out_specs=[
            pl.BlockSpec((tm, N), lambda s: (jnp.maximum(s - 1, 0), 0)),
            pl.BlockSpec((N_GATE_COLS, tm), lambda s: (0, jnp.maximum(s - 1, 0))),
        ],
        out_shape=[
            jax.ShapeDtypeStruct((T, N), BF16),
            jax.ShapeDtypeStruct((N_GATE_COLS, T), F32),
        ],
        scratch_shapes=[pltpu.VMEM((2, tm, D), BF16)],
        compiler_params=_cparams(("arbitrary",)),
        name="inproj",
    )(x2, norm_g, w_main, w_gates, b_gates, cos, sin_a, sin_b, q_gain, k_gain)


def _attn_kernel(q_ref, k_ref, v_ref, z_ref, o_ref, vx_ref, *, tq):
    qi = pl.program_id(2)

    @pl.when(qi == 0)
    def _():
        vx_ref[:, :HEAD_DIM] = v_ref[...]
        vx_ref[:, HEAD_DIM:] = jnp.ones(v_ref.shape, BF16)

    block_rows = q_ref.shape[0]
    units = [(t, g) for t in range(block_rows // tq) for g in range(GROUP)]

    def scores(unit):
        t, g = unit
        q = q_ref[t * tq:(t + 1) * tq, g * HEAD_DIM:(g + 1) * HEAD_DIM]
        return lax.dot_general(q, k_ref[...], _NT, preferred_element_type=F32)

    s_next = scores(units[0])
    for i, (t, g) in enumerate(units):
        rows = slice(t * tq, (t + 1) * tq)
        cols = slice(g * HEAD_DIM, (g + 1) * HEAD_DIM)
        s = s_next
        if i + 1 < len(units):
            s_next = scores(units[i + 1])
        m = jnp.max(s, axis=-1, keepdims=True)
        p = jnp.exp2(s - m).astype(BF16)
        ox = jnp.dot(p, vx_ref[...], preferred_element_type=F32)
        o = ox[:, :HEAD_DIM] / ox[:, HEAD_DIM:]
        z = z_ref[rows, cols].astype(F32)
        o_ref[rows, cols] = (o * _silu(z)).astype(BF16)


def _attention(proj, *, B, S, tq, tiles_per_step):
    T = B * S
    bq = tq * tiles_per_step
    nq = S // bq
    gw = GROUP * HEAD_DIM
    return pl.pallas_call(
        functools.partial(_attn_kernel, tq=tq),
        grid=(B, ATTN_KV_HEADS, nq),
        in_specs=[
            pl.BlockSpec((bq, gw), lambda b, kv, qi: (b * nq + qi, OFF_AQ // gw + kv)),
            pl.BlockSpec((S, HEAD_DIM), lambda b, kv, qi: (b, OFF_AK // HEAD_DIM + kv)),
            pl.BlockSpec((S, HEAD_DIM), lambda b, kv, qi: (b, OFF_AV // HEAD_DIM + kv)),
            pl.BlockSpec((bq, gw), lambda b, kv, qi: (b * nq + qi, OFF_AZ // gw + kv)),
        ],
        out_specs=pl.BlockSpec((bq, gw), lambda b, kv, qi: (b * nq + qi, kv)),
        out_shape=jax.ShapeDtypeStruct((T, ATTN_WIDTH), BF16),
        scratch_shapes=[pltpu.VMEM((S, 2 * HEAD_DIM), BF16)],
        compiler_params=_cparams(("parallel", "parallel", "arbitrary")),
        name="attention",
    )(proj, proj, proj, proj)


def _split3(x):
    hi = x.astype(BF16)
    r1 = x - hi.astype(F32)
    mid = r1.astype(BF16)
    lo = (r1 - mid.astype(F32)).astype(BF16)
    return hi, mid, lo


def _log_sigmoid(x):
    return jnp.minimum(x, 0.0) - jnp.log1p(jnp.exp(-jnp.abs(x)))


def _cummax_lanes(x, reverse):
    n = x.shape[-1]
    lane = lax.broadcasted_iota(jnp.int32, x.shape, 1)
    k = 1
    while k < n:
        if reverse:
            shifted = jnp.where(lane < n - k, pltpu.roll(x, n - k, axis=1), -jnp.inf)
        else:
            shifted = jnp.where(lane >= k, pltpu.roll(x, k, axis=1), -jnp.inf)
        x = jnp.maximum(x, shifted)
        k *= 2
    return x


def _to_columns(row, n):
    return jnp.transpose(jnp.broadcast_to(row, (n, n)))


FWD, BWD = 0, 1
ROW_B, ROW_D2, ROW_CM, ROW_CM2, ROW_WE, ROW_G, ROW_ML = range(7)
N_ROW_KINDS = 7
LOG2E = math.log2(math.e)


def _gate_rows_kernel(g_ref, out_ref):
    L = CHUNK
    n = g_ref.shape[1]
    t_idx = lax.broadcasted_iota(jnp.int32, (L, L), 0)
    s_idx = lax.broadcasted_iota(jnp.int32, (L, L), 1)
    for dr in (FWD, BWD):
        i_rows = g_ref[2 * dr]
        lf = _log_sigmoid(g_ref[2 * dr + 1])
        tri = jnp.where(t_idx >= s_idx if dr == BWD else t_idx <= s_idx, 1.0, 0.0).astype(BF16)
        b = sum(jnp.dot(p, tri, preferred_element_type=F32) for p in _split3(lf))
        d = i_rows - b
        g = b[:, 0:1] if dr == BWD else b[:, L - 1:L]
        e = g + d
        mloc = jnp.max(e, axis=-1, keepdims=True)
        out_ref[dr, ROW_B] = b
        cm = _cummax_lanes(d, reverse=(dr == BWD))
        out_ref[dr, ROW_D2] = d * LOG2E
        out_ref[dr, ROW_CM] = cm
        out_ref[dr, ROW_CM2] = cm * LOG2E
        out_ref[dr, ROW_WE] = jnp.exp(e - mloc)
        out_ref[dr, ROW_G] = jnp.broadcast_to(g, (n, L))
        out_ref[dr, ROW_ML] = jnp.broadcast_to(mloc, (n, L))


def _gate_rows(gates_rows):
    n_rows = gates_rows.shape[1]
    rb = math.gcd(n_rows, 512)
    return pl.pallas_call(
        _gate_rows_kernel,
        grid=(n_rows // rb,),
        in_specs=[pl.BlockSpec((4, rb, CHUNK), lambda i: (0, i, 0))],
        out_specs=pl.BlockSpec((2, N_ROW_KINDS, rb, CHUNK), lambda i: (0, 0, i, 0)),
        out_shape=jax.ShapeDtypeStruct((2, N_ROW_KINDS, n_rows, CHUNK), F32),
        compiler_params=_cparams(("parallel",)),
        name="gate_rows",
    )(gates_rows)


def _mlstm_kernel(q_ref, k_ref, v_ref, o_ref, z_ref, row_ref, ng_ref, zero_ref, out_ref,
                  g1_ref, g2_ref, mix_ref, coef_ref, den_ref, u_ref, nu_ref, c_ref, n_ref, *, nc):
    L = CHUNK
    scale = MLSTM_QK_DIM ** -0.5
    B_, D2_, CM_, CM2_, WE_, G_, ML_ = ROW_B, ROW_D2, ROW_CM, ROW_CM2, ROW_WE, ROW_G, ROW_ML

    t_idx = lax.broadcasted_iota(jnp.int32, (L, L), 0)
    s_idx = lax.broadcasted_iota(jnp.int32, (L, L), 1)
    lower = s_idx <= t_idx
    upper = s_idx >= t_idx
    eye = jnp.where(s_idx == t_idx, 1.0, 0.0).astype(BF16)

    ones8 = jnp.ones((8, L), BF16)

    def chunk_rows(c):
        return slice(c * L, (c + 1) * L)

    def intra_group(gi, carry):
        cs = [gi * group + j for j in range(group)]
        rows = [chunk_rows(c) for c in cs]
        kts = [jnp.transpose(k_ref[r, :].astype(F32)).astype(BF16) for r in rows]
        scores = [jnp.dot(q_ref[r, :], kt, preferred_element_type=F32) for r, kt in zip(rows, kts)]
        tied = []
        for r in rows:
            g1 = _sigmoid(o_ref[r, :].astype(F32))
            g2 = ng_ref[...] * _silu(z_ref[r, :].astype(F32))
            g1_ref[r, :] = g1
            g2_ref[r, :] = g2
            bits = pltpu.bitcast(g1, jnp.int32) | pltpu.bitcast(g2, jnp.int32)
            bits = bits[:, :L] | bits[:, L:]
            acc = bits[0:8, :]
            for i in range(1, L // 8):
                acc = acc | bits[8 * i:8 * i + 8, :]
            tied.append((acc & zero_ref[...]).astype(F32)[0:1, :])
        for c, r, kt, tie in zip(cs, rows, kts, tied):
            we = [row_ref[dr, WE_, pl.ds(c, 1), :] + tie for dr in (FWD, BWD)]
            kts2 = jnp.concatenate([kt * w.astype(BF16) for w in we], axis=0)
            u2 = jnp.dot(kts2, v_ref[r, :], preferred_element_type=F32)
            we16 = jnp.concatenate([jnp.broadcast_to(w, (8, L)) for w in we], axis=0).astype(BF16)
            nu2 = jnp.dot(we16, k_ref[r, :], preferred_element_type=F32)
            for dr in (FWD, BWD):
                u_ref[dr, c] = u2[dr * MLSTM_QK_DIM:(dr + 1) * MLSTM_QK_DIM, :]
                nu_ref[dr, c] = nu2[dr * 8:(dr + 1) * 8, :]
        qks = []
        for c, s in zip(cs, scores):
            pair_qk = []
            for dr in (FWD, BWD):
                drow = row_ref[dr, D2_, pl.ds(c, 1), :]
                cm_col = _to_columns(row_ref[dr, CM2_, pl.ds(c, 1), :], L)
                w = jnp.exp2(jnp.where(upper if dr == BWD else lower, drow - cm_col, -jnp.inf))
                pair_qk.append((s * w).astype(BF16))
            qks.append(jnp.concatenate(pair_qk, axis=0))
        for c, r, qk2 in zip(cs, rows, qks):
            num2 = jnp.dot(qk2, v_ref[r, :], preferred_element_type=F32).astype(BF16)
            den2 = lax.dot_general(ones8, qk2, _NT, preferred_element_type=F32)
            for dr in (FWD, BWD):
                mix_ref[c, 2 * dr * L:(2 * dr + 1) * L, :] = num2[dr * L:(dr + 1) * L, :]
                den_ref[dr, c] = den2[:, dr * L:(dr + 1) * L]
        return carry
    group = math.gcd(nc, 2)
    for gi in range(nc // group):
        intra_group(gi, 0)

    c_ref[...] = jnp.zeros_like(c_ref)
    n_ref[...] = jnp.zeros_like(n_ref)

    def inter_group(gi, carry):
        m = list(carry)
        chains = []
        for j in range(pair):
            for dr in (FWD, BWD):
                c = gi * pair + j if dr == FWD else nc - 1 - (gi * pair + j)
                brow = row_ref[dr, B_, pl.ds(c, 1), :]
                inter_ = brow + m[dr]
                amax = brow + row_ref[dr, CM_, pl.ds(c, 1), :]
                m_t = jnp.maximum(inter_, amax)
                al = jnp.exp(amax - m_t) * scale
                be = jnp.exp(inter_ - m_t) * scale
                floor = jnp.exp(-m_t)
                g = row_ref[dr, G_, pl.ds(c, 1), :]
                ml = row_ref[dr, ML_, pl.ds(c, 1), :]
                m_new = jnp.maximum(g + m[dr], ml)
                sc = jnp.exp(g + m[dr] - m_new)
                su = jnp.exp(ml - m_new)
                c_old = c_ref[dr]
                n_old = n_ref[dr]
                c_ref[dr] = (jnp.concatenate([sc, sc], axis=1) * c_old
                             + jnp.concatenate([su, su], axis=1) * u_ref[dr, c])
                n_ref[dr] = sc * n_old + su * nu_ref[dr, c]
                m[dr] = m_new
                chains.append((dr, c, al, be, floor, c_old.astype(BF16), n_old.astype(BF16)))
        prods = []
        for dr, c, al, be, floor, c_bf, n_bf in chains:
            qc = q_ref[chunk_rows(c), :]
            qn = lax.dot_general(n_bf, qc, _NT, preferred_element_type=F32)[0:1]
            prods.append((qn, jnp.dot(qc, c_bf, preferred_element_type=F32)))
        for (dr, c, al, be, floor, _, _), (qn, q_c) in zip(chains, prods):
            den = al * den_ref[dr, c][0:1] + be * qn
            r = 1.0 / jnp.maximum(jnp.abs(den), floor)
            coef_ref[c, 2 * dr:2 * dr + 1, :] = al * r
            coef_ref[c, 2 * dr + 1:2 * dr + 2, :] = be * r
            mix_ref[c, (2 * dr + 1) * L:(2 * dr + 2) * L, :] = q_c.astype(BF16)
        return tuple(m)

    pair = math.gcd(nc, 4)
    zero = jnp.zeros((1, L), F32)
    m_carry = (zero, zero)
    for gi in range(nc // pair):
        m_carry = inter_group(gi, m_carry)

    def fin_group(gi, carry):
        cs = [gi * fin + j for j in range(fin)]
        hs = []
        for c in cs:
            diags = [eye * jnp.broadcast_to(coef_ref[c, x:x + 1, :].astype(BF16), (L, L)) for x in range(4)]
            hs.append(jnp.dot(jnp.concatenate(diags, axis=1), mix_ref[c], preferred_element_type=F32))
        for c, h in zip(cs, hs):
            rows = chunk_rows(c)
            hm = h * g1_ref[rows, :]
            ms = jnp.mean(hm * hm, axis=-1, keepdims=True)
            out_ref[rows, :] = (hm * lax.rsqrt(ms + EPS) * g2_ref[rows, :]).astype(BF16)
        return carry
    fin = math.gcd(nc, 4)
    for gi in range(nc // fin):
        fin_group(gi, 0)


def _mlstm(proj, rows, norm_gain, *, B, S):
    T = B * S
    nc = S // CHUNK
    dk, dv = MLSTM_QK_DIM, MLSTM_V_DIM
    return pl.pallas_call(
        functools.partial(_mlstm_kernel, nc=nc),
        grid=(B, MLSTM_HEADS),
        in_specs=[
            pl.BlockSpec((S, dk), lambda b, h: (b, OFF_MQ // dk + h)),
            pl.BlockSpec((S, dk), lambda b, h: (b, OFF_MK // dk + h)),
            pl.BlockSpec((S, dv), lambda b, h: (b, OFF_MV // dv + h)),
            pl.BlockSpec((S, dv), lambda b, h: (b, OFF_MO // dv + h)),
            pl.BlockSpec((S, dv), lambda b, h: (b, OFF_MZ // dv + h)),
            pl.BlockSpec((2, N_ROW_KINDS, nc, CHUNK), lambda b, h: (0, 0, h * B + b, 0)),
            pl.BlockSpec((1, dv), lambda b, h: (0, h)),
            pl.BlockSpec((8, CHUNK), lambda b, h: (0, 0)),
        ],
        out_specs=pl.BlockSpec((S, dv), lambda b, h: (b, h)),
        out_shape=jax.ShapeDtypeStruct((T, MLSTM_WIDTH), BF16),
        scratch_shapes=[
            pltpu.VMEM((S, dv), F32),
            pltpu.VMEM((S, dv), F32),
            pltpu.VMEM((nc, 4 * CHUNK, dv), BF16),
            pltpu.VMEM((nc, 8, CHUNK), F32),
            pltpu.VMEM((2, nc, 8, CHUNK), F32),
            pltpu.VMEM((2, nc, dk, dv), F32),
            pltpu.VMEM((2, nc, 8, dk), F32),
            pltpu.VMEM((2, dk, dv), F32),
            pltpu.VMEM((2, 8, dk), F32),
        ],
        compiler_params=_cparams(("parallel", "arbitrary")),
        name="mlstm",
    )(proj, proj, proj, proj, proj, rows, norm_gain, jnp.zeros((8, CHUNK), jnp.int32))


def _outproj_kernel(a_ref, m_ref, wa_ref, wm_ref, x_ref, y_ref):
    y = jnp.dot(a_ref[...], wa_ref[...], preferred_element_type=F32)
    y = y + jnp.dot(m_ref[...], wm_ref[...], preferred_element_type=F32)
    y_ref[...] = x_ref[...] + y


def _out_projection(attn, mlstm, w_out, x2, *, tm):
    T, D = x2.shape
    return pl.pallas_call(
        _outproj_kernel,
        grid=(T // tm,),
        in_specs=[
            pl.BlockSpec((tm, ATTN_WIDTH), lambda i: (i, 0)),
            pl.BlockSpec((tm, MLSTM_WIDTH), lambda i: (i, 0)),
            pl.BlockSpec((ATTN_WIDTH, D), lambda i: (0, 0)),
            pl.BlockSpec((MLSTM_WIDTH, D), lambda i: (1, 0)),
            pl.BlockSpec((tm, D), lambda i: (i, 0)),
        ],
        out_specs=pl.BlockSpec((tm, D), lambda i: (i, 0)),
        out_shape=jax.ShapeDtypeStruct((T, D), F32),
        compiler_params=_cparams(("parallel",)),
        name="outproj",
    )(attn, mlstm, w_out, w_out, x2)


def _rope_tables(S):
    row = (jnp.arange(S) // GRID_W).astype(F32)
    col = (jnp.arange(S) % GRID_W).astype(F32)
    nf = HEAD_DIM // 4
    inv = 1.0 / (ROPE_THETA ** (jnp.arange(nf, dtype=F32) / nf))
    ang_r = row[:, None] * inv
    ang_c = col[:, None] * inv
    ang = jnp.concatenate([ang_r, ang_r, ang_c, ang_c], axis=-1)
    cos, sin = jnp.cos(ang), jnp.sin(ang)
    quarter = (jnp.arange(HEAD_DIM) // nf) % 2
    sin_a = jnp.where(quarter == 0, -sin, 0.0)
    sin_b = jnp.where(quarter == 1, sin, 0.0)
    return cos, sin_a, sin_b


def _cast_t_kernel(wt_ref, o_ref):
    for r in range(wt_ref.shape[0] // LANES):
        rows = slice(r * LANES, (r + 1) * LANES)
        o_ref[:, rows] = jnp.transpose(wt_ref[rows, :]).astype(BF16)


def _cast_transposed(w_t, *, n_cols, block):
    D = w_t.shape[1]
    return pl.pallas_call(
        _cast_t_kernel,
        grid=(n_cols // block,),
        in_specs=[pl.BlockSpec((block, D), lambda j: (j, 0))],
        out_specs=pl.BlockSpec((D, block), lambda j: (0, j)),
        out_shape=jax.ShapeDtypeStruct((D, n_cols), BF16),
        compiler_params=_cparams(("parallel",)),
        name="cast_weights",
    )(w_t)


def _layer(x, norm_g, w_in, b_gates, q_norm_g, k_norm_g, mlstm_norm_g, w_out):
    B, S, D = x.shape
    T = B * S
    nc = S // CHUNK
    x2 = x.reshape(T, D)

    w_in_t = w_in.T
    w_main = _cast_transposed(w_in_t, n_cols=N_MAIN, block=512)
    w_gates = w_in_t[N_MAIN:].astype(BF16)
    tm = math.gcd(S, 256)
    bias = b_gates.astype(F32).reshape(N_GATE_COLS, 1)
    cos, sin_a, sin_b = _rope_tables(S)
    proj, gates_t = _in_projection(x2, norm_g.reshape(1, D), w_main, w_gates, bias, cos, sin_a, sin_b,
                                   q_norm_g.reshape(1, HEAD_DIM), k_norm_g.reshape(1, HEAD_DIM),
                                   tm=tm, tn=512, seq_len=S)

    attn = _attention(proj, B=B, S=S, tq=math.gcd(S, 256), tiles_per_step=8 if S % 2048 == 0 else 1)

    gates_rows = gates_t.reshape(4, MLSTM_HEADS * B * nc, CHUNK)
    mlstm = _mlstm(proj, _gate_rows(gates_rows), mlstm_norm_g.reshape(1, MLSTM_WIDTH), B=B, S=S)

    y = _out_projection(attn, mlstm, w_out.astype(BF16), x2, tm=math.gcd(T, 512))
    return y.reshape(B, S, D)


def kernel(x_prompt, x_sample, norm_g, w_in, b_gates, q_norm_g, k_norm_g, mlstm_norm_g, w_out):
    depth = norm_g.shape[0]
    outs = []
    for x in (x_prompt, x_sample):
        for l in range(depth):
            x = _layer(x, norm_g[l], w_in[l], b_gates[l], q_norm_g[l], k_norm_g[l],
                       mlstm_norm_g[l], w_out[l])
        outs.append(x)
    return tuple(outs)
```

```python
import functools
import math

import jax
import jax.numpy as jnp
from jax import lax
from jax.experimental import pallas as pl
from jax.experimental.pallas import tpu as pltpu

F32 = jnp.float32
BF16 = jnp.bfloat16

ATTN_HEADS = 8
ATTN_KV_HEADS = 2
HEAD_DIM = 128
ATTN_WIDTH = ATTN_HEADS * HEAD_DIM
KV_WIDTH = ATTN_KV_HEADS * HEAD_DIM
GROUP = ATTN_HEADS // ATTN_KV_HEADS
MLSTM_HEADS = 4
MLSTM_QK_DIM = 128
MLSTM_V_DIM = 256
MLSTM_QK_WIDTH = MLSTM_HEADS * MLSTM_QK_DIM
MLSTM_WIDTH = MLSTM_HEADS * MLSTM_V_DIM
N_GATE_COLS = 4 * MLSTM_HEADS
GRID_W = 64
ROPE_THETA = 10000.0
CHUNK = 128
EPS = 1e-6

OFF_AQ = 0
OFF_AK = OFF_AQ + ATTN_WIDTH
OFF_AV = OFF_AK + KV_WIDTH
OFF_AZ = OFF_AV + KV_WIDTH
OFF_MQ = OFF_AZ + ATTN_WIDTH
OFF_MK = OFF_MQ + MLSTM_QK_WIDTH
OFF_MV = OFF_MK + MLSTM_QK_WIDTH
OFF_MO = OFF_MV + MLSTM_WIDTH
OFF_MZ = OFF_MO + MLSTM_WIDTH
OFF_GATES = OFF_MZ + MLSTM_WIDTH
N_MAIN = OFF_GATES

LANES = 128
VMEM_LIMIT = 56 * 1024 * 1024


def _cparams(sem):
    return pltpu.CompilerParams(dimension_semantics=sem, vmem_limit_bytes=VMEM_LIMIT)


_NT = (((1,), (1,)), ((), ()))


def _sigmoid(x):
    return 0.5 * jnp.tanh(0.5 * x) + 0.5


def _silu(x):
    hx = 0.5 * x
    return hx * jnp.tanh(hx) + hx


def _norm_rope(x, gain, cos, sin_a, sin_b):
    ms = jnp.mean(x * x, axis=-1, keepdims=True)
    xn = x * lax.rsqrt(ms + EPS) * gain
    up = pltpu.roll(xn, 3 * HEAD_DIM // 4, axis=1)
    down = pltpu.roll(xn, HEAD_DIM // 4, axis=1)
    return xn * cos + up * sin_a + down * sin_b


def _inproj_kernel(x_ref, g_ref, w_ref, wg_ref, bg_ref, cos_ref, sa_ref, sb_ref, qg_ref, kg_ref,
                   o_ref, og_ref, h_ref, *, tn, row_chunk, seq_len):
    s = pl.program_id(0)
    tm, _ = x_ref.shape
    n_tiles = w_ref.shape[1] // tn
    n_chunks = tm // row_chunk
    nxt = s % 2
    cur = 1 - nxt

    @pl.when(s == 0)
    def _():
        h_ref[1] = jnp.zeros(h_ref.shape[1:], BF16)

    def normalise(r):
        rows = slice(r * row_chunk, (r + 1) * row_chunk)
        x = x_ref[rows, :]
        ms = jnp.mean(x * x, axis=-1, keepdims=True)
        h_ref[nxt, rows, :] = (x * lax.rsqrt(ms + EPS) * g_ref[...]).astype(BF16)

    pos = pl.ds(pl.multiple_of((jnp.maximum(s - 1, 0) * tm) % seq_len, tm), tm)
    qscale = math.log2(math.e) / math.sqrt(HEAD_DIM)

    first_plain = -(-OFF_AZ // tn)
    per_tile = -(-n_chunks // (n_tiles - first_plain))
    for j in range(n_tiles):
        if j == n_tiles // 2:
            og_ref[...] = (lax.dot_general(wg_ref[...], h_ref[cur], _NT, preferred_element_type=F32)
                           + bg_ref[...])
        acc = jnp.dot(h_ref[cur], w_ref[:, j * tn:(j + 1) * tn], preferred_element_type=F32)
        for c0 in range(j * tn, (j + 1) * tn, HEAD_DIM):
            piece = acc[:, c0 - j * tn:c0 - j * tn + HEAD_DIM]
            if c0 < OFF_AK:
                piece = _norm_rope(piece, qg_ref[...], cos_ref[pos, :], sa_ref[pos, :], sb_ref[pos, :]) * qscale
            elif c0 < OFF_AV:
                piece = _norm_rope(piece, kg_ref[...], cos_ref[pos, :], sa_ref[pos, :], sb_ref[pos, :])
            o_ref[:, c0:c0 + HEAD_DIM] = piece.astype(BF16)
        if j >= first_plain:
            for r in range((j - first_plain) * per_tile, min((j - first_plain + 1) * per_tile, n_chunks)):
                normalise(r)


def _in_projection(x2, norm_g, w_main, w_gates, b_gates, cos, sin_a, sin_b, q_gain, k_gain, *, tm, tn, seq_len):
    T, D = x2.shape
    N = w_main.shape[1]
    n_blocks = T // tm
    resident = pl.Buffered(1)
    const = lambda s: (0, 0)
    return pl.pallas_call(
        functools.partial(_inproj_kernel, tn=tn, row_chunk=32, seq_len=seq_len),
        grid=(n_blocks + 1,),
        in_specs=[
            pl.BlockSpec((tm, D), lambda s: (jnp.minimum(s, n_blocks - 1), 0)),
            pl.BlockSpec((1, D), const),
            pl.BlockSpec((D, N_MAIN), const, pipeline_mode=resident),
            pl.BlockSpec((N_GATE_COLS, D), const),
            pl.BlockSpec((N_GATE_COLS, 1), const),
            pl.BlockSpec((seq_len, HEAD_DIM), const, pipeline_mode=resident),
            pl.BlockSpec((seq_len, HEAD_DIM), const, pipeline_mode=resident),
            pl.BlockSpec((seq_len, HEAD_DIM), const, pipeline_mode=resident),
            pl.BlockSpec((1, HEAD_DIM), const),
            pl.BlockSpec((1, HEAD_DIM), const),
        ],
        out_specs=[
            pl.BlockSpec((tm, N), lambda s: (jnp.maximum(s - 1, 0), 0)),
            pl.BlockSpec((N_GATE_COLS, tm), lambda s: (0, jnp.maximum(s - 1, 0))),
        ],
        out_shape=[
            jax.ShapeDtypeStruct((T, N), BF16),
            jax.ShapeDtypeStruct((N_GATE_COLS, T), F32),
        ],
        scratch_shapes=[pltpu.VMEM((2, tm, D), BF16)],
        compiler_params=_cparams(("arbitrary",)),
        name="inproj",
    )(x2, norm_g, w_main, w_gates, b_gates, cos, sin_a, sin_b, q_gain, k_gain)


def _attn_kernel(q_ref, k_ref, v_ref, z_ref, o_ref, vx_ref, *, tq):
    qi = pl.program_id(2)

    @pl.when(qi == 0)
    def _():
        vx_ref[:, :HEAD_DIM] = v_ref[...]
        vx_ref[:, HEAD_DIM:] = jnp.ones(v_ref.shape, BF16)

    block_rows = q_ref.shape[0]
    units = [(t, g) for t in range(block_rows // tq) for g in range(GROUP)]

    def scores(unit):
        t, g = unit
        q = q_ref[t * tq:(t + 1) * tq, g * HEAD_DIM:(g + 1) * HEAD_DIM]
        return lax.dot_general(q, k_ref[...], _NT, preferred_element_type=F32)

    s_next = scores(units[0])
    for i, (t, g) in enumerate(units):
        rows = slice(t * tq, (t + 1) * tq)
        cols = slice(g * HEAD_DIM, (g + 1) * HEAD_DIM)
        s = s_next
        if i + 1 < len(units):
            s_next = scores(units[i + 1])
        m = jnp.max(s, axis=-1, keepdims=True)
        p = jnp.exp2(s - m).astype(BF16)
        ox = jnp.dot(p, vx_ref[...], preferred_element_type=F32)
        o = ox[:, :HEAD_DIM] / ox[:, HEAD_DIM:]
        z = z_ref[rows, cols].astype(F32)
        o_ref[rows, cols] = (o * _silu(z)).astype(BF16)


def _attention(proj, *, B, S, tq, tiles_per_step):
    T = B * S
    bq = tq * tiles_per_step
    nq = S // bq
    gw = GROUP * HEAD_DIM
    return pl.pallas_call(
        functools.partial(_attn_kernel, tq=tq),
        grid=(B, ATTN_KV_HEADS, nq),
        in_specs=[
            pl.BlockSpec((bq, gw), lambda b, kv, qi: (b * nq + qi, OFF_AQ // gw + kv)),
            pl.BlockSpec((S, HEAD_DIM), lambda b, kv, qi: (b, OFF_AK // HEAD_DIM + kv)),
            pl.BlockSpec((S, HEAD_DIM), lambda b, kv, qi: (b, OFF_AV // HEAD_DIM + kv)),
            pl.BlockSpec((bq, gw), lambda b, kv, qi: (b * nq + qi, OFF_AZ // gw + kv)),
        ],
        out_specs=pl.BlockSpec((bq, gw), lambda b, kv, qi: (b * nq + qi, kv)),
        out_shape=jax.ShapeDtypeStruct((T, ATTN_WIDTH), BF16),
        scratch_shapes=[pltpu.VMEM((S, 2 * HEAD_DIM), BF16)],
        compiler_params=_cparams(("parallel", "parallel", "arbitrary")),
        name="attention",
    )(proj, proj, proj, proj)


def _split3(x):
    hi = x.astype(BF16)
    r1 = x - hi.astype(F32)
    mid = r1.astype(BF16)
    lo = (r1 - mid.astype(F32)).astype(BF16)
    return hi, mid, lo


def _log_sigmoid(x):
    return jnp.minimum(x, 0.0) - jnp.log1p(jnp.exp(-jnp.abs(x)))


def _cummax_lanes(x, reverse):
    n = x.shape[-1]
    lane = lax.broadcasted_iota(jnp.int32, x.shape, 1)
    k = 1
    while k < n:
        if reverse:
            shifted = jnp.where(lane < n - k, pltpu.roll(x, n - k, axis=1), -jnp.inf)
        else:
            shifted = jnp.where(lane >= k, pltpu.roll(x, k, axis=1), -jnp.inf)
        x = jnp.maximum(x, shifted)
        k *= 2
    return x


def _to_columns(row, n):
    return jnp.transpose(jnp.broadcast_to(row, (n, n)))


FWD, BWD = 0, 1
ROW_B, ROW_D2, ROW_CM, ROW_CM2, ROW_WE, ROW_G, ROW_ML = range(7)
N_ROW_KINDS = 7
LOG2E = math.log2(math.e)


def _gate_rows_kernel(g_ref, out_ref):
    L = CHUNK
    n = g_ref.shape[1]
    t_idx = lax.broadcasted_iota(jnp.int32, (L, L), 0)
    s_idx = lax.broadcasted_iota(jnp.int32, (L, L), 1)
    for dr in (FWD, BWD):
        i_rows = g_ref[2 * dr]
        lf = _log_sigmoid(g_ref[2 * dr + 1])
        tri = jnp.where(t_idx >= s_idx if dr == BWD else t_idx <= s_idx, 1.0, 0.0).astype(BF16)
        b = sum(jnp.dot(p, tri, preferred_element_type=F32) for p in _split3(lf))
        d = i_rows - b
        g = b[:, 0:1] if dr == BWD else b[:, L - 1:L]
        e = g + d
        mloc = jnp.max(e, axis=-1, keepdims=True)
        out_ref[dr, ROW_B] = b
        cm = _cummax_lanes(d, reverse=(dr == BWD))
        out_ref[dr, ROW_D2] = d * LOG2E
        out_ref[dr, ROW_CM] = cm
        out_ref[dr, ROW_CM2] = cm * LOG2E
        out_ref[dr, ROW_WE] = jnp.exp(e - mloc)
        out_ref[dr, ROW_G] = jnp.broadcast_to(g, (n, L))
        out_ref[dr, ROW_ML] = jnp.broadcast_to(mloc, (n, L))


def _gate_rows(gates_rows):
    n_rows = gates_rows.shape[1]
    rb = math.gcd(n_rows, 512)
    return pl.pallas_call(
        _gate_rows_kernel,
        grid=(n_rows // rb,),
        in_specs=[pl.BlockSpec((4, rb, CHUNK), lambda i: (0, i, 0))],
        out_specs=pl.BlockSpec((2, N_ROW_KINDS, rb, CHUNK), lambda i: (0, 0, i, 0)),
        out_shape=jax.ShapeDtypeStruct((2, N_ROW_KINDS, n_rows, CHUNK), F32),
        compiler_params=_cparams(("parallel",)),
        name="gate_rows",
    )(gates_rows)


def _mlstm_kernel(q_ref, k_ref, v_ref, o_ref, z_ref, row_ref, ng_ref, zero_ref, out_ref,
                  g1_ref, g2_ref, mix_ref, coef_ref, den_ref, u_ref, nu_ref, c_ref, n_ref, *, nc):
    L = CHUNK
    scale = MLSTM_QK_DIM ** -0.5
    B_, D2_, CM_, CM2_, WE_, G_, ML_ = ROW_B, ROW_D2, ROW_CM, ROW_CM2, ROW_WE, ROW_G, ROW_ML

    t_idx = lax.broadcasted_iota(jnp.int32, (L, L), 0)
    s_idx = lax.broadcasted_iota(jnp.int32, (L, L), 1)
    lower = s_idx <= t_idx
    upper = s_idx >= t_idx
    eye = jnp.where(s_idx == t_idx, 1.0, 0.0).astype(BF16)

    ones8 = jnp.ones((8, L), BF16)

    def chunk_rows(c):
        return slice(c * L, (c + 1) * L)

    def intra_group(gi, carry):
        cs = [gi * group + j for j in range(group)]
        rows = [chunk_rows(c) for c in cs]
        kts = [jnp.transpose(k_ref[r, :].astype(F32)).astype(BF16) for r in rows]
        scores = [jnp.dot(q_ref[r, :], kt, preferred_element_type=F32) for r, kt in zip(rows, kts)]
        tied = []
        for r in rows:
            g1 = _sigmoid(o_ref[r, :]).astype(F32)
            g2 = ng_ref[...] * _silu(z_ref[r, :]).astype(F32)
            g1_ref[r, :] = g1
            g2_ref[r, :] = g2
            bits = pltpu.bitcast(g1, jnp.int32) | pltpu.bitcast(g2, jnp.int32)
            bits = bits[:, :L] | bits[:, L:]
            acc = bits[0:8, :]
            for i in range(1, L // 8):
                acc = acc | bits[8 * i:8 * i + 8, :]
            tied.append((acc & zero_ref[...]).astype(F32)[0:1, :])
        for c, r, kt, tie in zip(cs, rows, kts, tied):
            we = [row_ref[dr, WE_, pl.ds(c, 1), :] + tie for dr in (FWD, BWD)]
            kts2 = jnp.concatenate([kt * w.astype(BF16) for w in we], axis=0)
            u2 = jnp.dot(kts2, v_ref[r, :], preferred_element_type=F32)
            we16 = jnp.concatenate([jnp.broadcast_to(w, (8, L)) for w in we], axis=0).astype(BF16)
            nu2 = jnp.dot(we16, k_ref[r, :], preferred_element_type=F32)
            for dr in (FWD, BWD):
                u_ref[dr, c] = u2[dr * MLSTM_QK_DIM:(dr + 1) * MLSTM_QK_DIM, :]
                nu_ref[dr, c] = nu2[dr * 8:(dr + 1) * 8, :]
        qks = []
        for c, s in zip(cs, scores):
            pair_qk = []
            for dr in (FWD, BWD):
                drow = row_ref[dr, D2_, pl.ds(c, 1), :]
                cm_col = _to_columns(row_ref[dr, CM2_, pl.ds(c, 1), :], L)
                w = jnp.exp2(jnp.where(upper if dr == BWD else lower, drow - cm_col, -jnp.inf))
                pair_qk.append((s * w).astype(BF16))
            qks.append(jnp.concatenate(pair_qk, axis=0))
        for c, r, qk2 in zip(cs, rows, qks):
            num2 = jnp.dot(qk2, v_ref[r, :], preferred_element_type=F32).astype(BF16)
            den2 = lax.dot_general(ones8, qk2, _NT, preferred_element_type=F32)
            for dr in (FWD, BWD):
                mix_ref[c, 2 * dr * L:(2 * dr + 1) * L, :] = num2[dr * L:(dr + 1) * L, :]
                den_ref[dr, c] = den2[:, dr * L:(dr + 1) * L]
        return carry
    group = math.gcd(nc, 2)
    for gi in range(nc // group):
        intra_group(gi, 0)

    c_ref[...] = jnp.zeros_like(c_ref)
    n_ref[...] = jnp.zeros_like(n_ref)

    def inter_group(gi, carry):
        m = list(carry)
        chains = []
        for j in range(pair):
            for dr in (FWD, BWD):
                c = gi * pair + j if dr == FWD else nc - 1 - (gi * pair + j)
                brow = row_ref[dr, B_, pl.ds(c, 1), :]
                inter_ = brow + m[dr]
                amax = brow + row_ref[dr, CM_, pl.ds(c, 1), :]
                m_t = jnp.maximum(inter_, amax)
                al = jnp.exp(amax - m_t) * scale
                be = jnp.exp(inter_ - m_t) * scale
                floor = jnp.exp(-m_t)
                g = row_ref[dr, G_, pl.ds(c, 1), :]
                ml = row_ref[dr, ML_, pl.ds(c, 1), :]
                m_new = jnp.maximum(g + m[dr], ml)
                sc = jnp.exp(g + m[dr] - m_new)
                su = jnp.exp(ml - m_new)
                c_old = c_ref[dr]
                n_old = n_ref[dr]
                c_ref[dr] = (jnp.concatenate([sc, sc], axis=1) * c_old
                             + jnp.concatenate([su, su], axis=1) * u_ref[dr, c])
                n_ref[dr] = sc * n_old + su * nu_ref[dr, c]
                m[dr] = m_new
                chains.append((dr, c, al, be, floor, c_old.astype(BF16), n_old.astype(BF16)))
        prods = []
        for dr, c, al, be, floor, c_bf, n_bf in chains:
            qc = q_ref[chunk_rows(c), :]
            qn = lax.dot_general(n_bf, qc, _NT, preferred_element_type=F32)[0:1]
            prods.append((qn, jnp.dot(qc, c_bf, preferred_element_type=F32)))
        for (dr, c, al, be, floor, _, _), (qn, q_c) in zip(chains, prods):
            den = al * den_ref[dr, c][0:1] + be * qn
            r = 1.0 / jnp.maximum(jnp.abs(den), floor)
            coef_ref[c, 2 * dr:2 * dr + 1, :] = al * r
            coef_ref[c, 2 * dr + 1:2 * dr + 2, :] = be * r
            mix_ref[c, (2 * dr + 1) * L:(2 * dr + 2) * L, :] = q_c.astype(BF16)
        return tuple(m)

    pair = math.gcd(nc, 4)
    zero = jnp.zeros((1, L), F32)
    m_carry = (zero, zero)
    for gi in range(nc // pair):
        m_carry = inter_group(gi, m_carry)

    def fin_group(gi, carry):
        cs = [gi * fin + j for j in range(fin)]
        hs = []
        for c in cs:
            diags = [eye * jnp.broadcast_to(coef_ref[c, x:x + 1, :].astype(BF16), (L, L)) for x in range(4)]
            hs.append(jnp.dot(jnp.concatenate(diags, axis=1), mix_ref[c], preferred_element_type=F32))
        for c, h in zip(cs, hs):
            rows = chunk_rows(c)
            hm = h * g1_ref[rows, :]
            ms = jnp.mean(hm * hm, axis=-1, keepdims=True)
            out_ref[rows, :] = (hm * lax.rsqrt(ms + EPS) * g2_ref[rows, :]).astype(BF16)
        return carry
    fin = math.gcd(nc, 4)
    for gi in range(nc // fin):
        fin_group(gi, 0)


def _mlstm(proj, rows, norm_gain, *, B, S):
    T = B * S
    nc = S // CHUNK
    dk, dv = MLSTM_QK_DIM, MLSTM_V_DIM
    return pl.pallas_call(
        functools.partial(_mlstm_kernel, nc=nc),
        grid=(B, MLSTM_HEADS),
        in_specs=[
            pl.BlockSpec((S, dk), lambda b, h: (b, OFF_MQ // dk + h)),
            pl.BlockSpec((S, dk), lambda b, h: (b, OFF_MK // dk + h)),
            pl.BlockSpec((S, dv), lambda b, h: (b, OFF_MV // dv + h)),
            pl.BlockSpec((S, dv), lambda b, h: (b, OFF_MO // dv + h)),
            pl.BlockSpec((S, dv), lambda b, h: (b, OFF_MZ // dv + h)),
            pl.BlockSpec((2, N_ROW_KINDS, nc, CHUNK), lambda b, h: (0, 0, h * B + b, 0)),
            pl.BlockSpec((1, dv), lambda b, h: (0, h)),
            pl.BlockSpec((8, CHUNK), lambda b, h: (0, 0)),
        ],
        out_specs=pl.BlockSpec((S, dv), lambda b, h: (b, h)),
        out_shape=jax.ShapeDtypeStruct((T, MLSTM_WIDTH), BF16),
        scratch_shapes=[
            pltpu.VMEM((S, dv), F32),
            pltpu.VMEM((S, dv), F32),
            pltpu.VMEM((nc, 4 * CHUNK, dv), BF16),
            pltpu.VMEM((nc, 8, CHUNK), F32),
            pltpu.VMEM((2, nc, 8, CHUNK), F32),
            pltpu.VMEM((2, nc, dk, dv), F32),
            pltpu.VMEM((2, nc, 8, dk), F32),
            pltpu.VMEM((2, dk, dv), F32),
            pltpu.VMEM((2, 8, dk), F32),
        ],
        compiler_params=_cparams(("parallel", "arbitrary")),
        name="mlstm",
    )(proj, proj, proj, proj, proj, rows, norm_gain, jnp.zeros((8, CHUNK), jnp.int32))


def _outproj_kernel(a_ref, m_ref, wa_ref, wm_ref, x_ref, y_ref, *, tn):
    for j in range(x_ref.shape[1] // tn):
        cols = slice(j * tn, (j + 1) * tn)
        y = jnp.dot(a_ref[...], wa_ref[:, cols], preferred_element_type=F32)
        y = y + jnp.dot(m_ref[...], wm_ref[:, cols], preferred_element_type=F32)
        y_ref[:, cols] = x_ref[:, cols] + y


def _out_projection(attn, mlstm, w_out, x2, *, tm, tn):
    T, D = x2.shape
    resident = pl.Buffered(1)
    return pl.pallas_call(
        functools.partial(_outproj_kernel, tn=tn),
        grid=(T // tm,),
        in_specs=[
            pl.BlockSpec((tm, ATTN_WIDTH), lambda i: (i, 0)),
            pl.BlockSpec((tm, MLSTM_WIDTH), lambda i: (i, 0)),
            pl.BlockSpec((ATTN_WIDTH, D), lambda i: (0, 0), pipeline_mode=resident),
            pl.BlockSpec((MLSTM_WIDTH, D), lambda i: (1, 0), pipeline_mode=resident),
            pl.BlockSpec((tm, D), lambda i: (i, 0)),
        ],
        out_specs=pl.BlockSpec((tm, D), lambda i: (i, 0)),
        out_shape=jax.ShapeDtypeStruct((T, D), F32),
        compiler_params=_cparams(("parallel",)),
        name="outproj",
    )(attn, mlstm, w_out, w_out, x2)


def _rope_tables(S):
    row = (jnp.arange(S) // GRID_W).astype(F32)
    col = (jnp.arange(S) % GRID_W).astype(F32)
    nf = HEAD_DIM // 4
    inv = 1.0 / (ROPE_THETA ** (jnp.arange(nf, dtype=F32) / nf))
    ang_r = row[:, None] * inv
    ang_c = col[:, None] * inv
    ang = jnp.concatenate([ang_r, ang_r, ang_c, ang_c], axis=-1)
    cos, sin = jnp.cos(ang), jnp.sin(ang)
    quarter = (jnp.arange(HEAD_DIM) // nf) % 2
    sin_a = jnp.where(quarter == 0, -sin, 0.0)
    sin_b = jnp.where(quarter == 1, sin, 0.0)
    return cos, sin_a, sin_b


def _cast_t_kernel(wt_ref, o_ref):
    for r in range(wt_ref.shape[0] // LANES):
        rows = slice(r * LANES, (r + 1) * LANES)
        o_ref[:, rows] = jnp.transpose(wt_ref[rows, :]).astype(BF16)


def _cast_transposed(w_t, *, n_cols, block):
    D = w_t.shape[1]
    return pl.pallas_call(
        _cast_t_kernel,
        grid=(n_cols // block,),
        in_specs=[pl.BlockSpec((block, D), lambda j: (j, 0))],
        out_specs=pl.BlockSpec((D, block), lambda j: (0, j)),
        out_shape=jax.ShapeDtypeStruct((D, n_cols), BF16),
        compiler_params=_cparams(("parallel",)),
        name="cast_weights",
    )(w_t)


def _layer(x, norm_g, w_in, b_gates, q_norm_g, k_norm_g, mlstm_norm_g, w_out):
    B, S, D = x.shape
    T = B * S
    nc = S // CHUNK
    x2 = x.reshape(T, D)

    w_in_t = w_in.T
    w_main = _cast_transposed(w_in_t, n_cols=N_MAIN, block=512)
    w_gates = w_in_t[N_MAIN:].astype(BF16)
    tm = math.gcd(S, 256)
    bias = b_gates.astype(F32).reshape(N_GATE_COLS, 1)
    cos, sin_a, sin_b = _rope_tables(S)
    proj, gates_t = _in_projection(x2, norm_g.reshape(1, D), w_main, w_gates, bias, cos, sin_a, sin_b,
                                   q_norm_g.reshape(1, HEAD_DIM), k_norm_g.reshape(1, HEAD_DIM),
                                   tm=tm, tn=512, seq_len=S)

    attn = _attention(proj, B=B, S=S, tq=math.gcd(S, 256), tiles_per_step=8 if S % 2048 == 0 else 1)

    gates_rows = gates_t.reshape(4, MLSTM_HEADS * B * nc, CHUNK)
    mlstm = _mlstm(proj, _gate_rows(gates_rows), mlstm_norm_g.reshape(1, MLSTM_WIDTH), B=B, S=S)

    y = _out_projection(attn, mlstm, w_out.astype(BF16), x2, tm=math.gcd(T, 1024), tn=512)
    return y.reshape(B, S, D)


def kernel(x_prompt, x_sample, norm_g, w_in, b_gates, q_norm_g, k_norm_g, mlstm_norm_g, w_out):
    depth = norm_g.shape[0]
    outs = []
    for x in (x_prompt, x_sample):
        for l in range(depth):
            x = _layer(x, norm_g[l], w_in[l], b_gates[l], q_norm_g[l], k_norm_g[l],
                       mlstm_norm_g[l], w_out[l])
        outs.append(x)
    return tuple(outs)
```
